```python
import math
import jax, jax.numpy as jnp
from jax import lax
import numpy as np

D_MODEL = 1024
BATCH = 2
SEQ = 16384
DEPTH = 2

CTX_LEN = 256
GRID_W = 64
Q_BLOCK = 128
HEAD_DIM = 64
GROUP_WIDTH = D_MODEL // 4
EPS = 1e-6
ROPE_THETA = 10000.0
HALF = 0.5
N_MOD = 9

DA_HEADS = GROUP_WIDTH // HEAD_DIM
DA_QK_DIM = HEAD_DIM // 2
DA_V_DIM = HEAD_DIM
MLA_HEADS = GROUP_WIDTH // HEAD_DIM
MLA_Q_RANK = GROUP_WIDTH
MLA_KV_RANK = GROUP_WIDTH // 2
MLA_NOPE_DIM = HEAD_DIM
MLA_ROPE_DIM = HEAD_DIM // 2
MLA_V_DIM = HEAD_DIM
HG_HEADS = GROUP_WIDTH // HEAD_DIM
HG_EXPAND = 128
HG_FDIM = HG_HEADS * HG_EXPAND
HG_VDIM = GROUP_WIDTH // HG_HEADS
HG_CHUNK = 64
GQ_HEADS = GROUP_WIDTH // HEAD_DIM
GQ_KV_HEADS = 2
D_FF = 2816

A_COLS = 4 * DA_HEADS * DA_QK_DIM + DA_HEADS * DA_V_DIM
B_COLS = MLA_Q_RANK + MLA_KV_RANK + MLA_ROPE_DIM
C_COLS = 3 * HG_FDIM + 2 * GROUP_WIDTH
D_COLS = (GQ_HEADS + 2 * GQ_KV_HEADS) * HEAD_DIM
MIX_SPLITS = (A_COLS, B_COLS, C_COLS, D_COLS)
MIX_COLS = A_COLS + B_COLS + C_COLS + D_COLS

kernel_name = "hybrid_parallel_group_dit_block"


def _rms(x, w):
    xf = x.astype(jnp.float32)
    y = xf * lax.rsqrt(jnp.mean(xf * xf, axis=-1, keepdims=True) + EPS)
    return (y * w.astype(jnp.float32)).astype(x.dtype)


def _split_cols(z, sizes):
    idx = np.cumsum(np.array(sizes))[:-1].tolist()
    return jnp.split(z, idx, axis=-1)


def _heads(x, h):
    b, n, _ = x.shape
    return x.reshape(b, n, h, -1).transpose(0, 2, 1, 3)


def _merge(x):
    b, h, n, d = x.shape
    return x.transpose(0, 2, 1, 3).reshape(b, n, h * d)


def _axial_rope(n, rot_dim, dtype):
    rows = n // GRID_W
    r = jnp.repeat(jnp.arange(rows), GRID_W).astype(jnp.float32)
    col = jnp.tile(jnp.arange(GRID_W), rows).astype(jnp.float32)
    n_freq = rot_dim // 4
    inv = ROPE_THETA ** (-jnp.arange(n_freq, dtype=jnp.float32) / n_freq)
    ang = jnp.concatenate([r[:, None] * inv, col[:, None] * inv], axis=-1)
    return jnp.cos(ang).astype(dtype), jnp.sin(ang).astype(dtype)


def _rope(x, cos, sin):
    x1, x2 = jnp.split(x, 2, axis=-1)
    return jnp.concatenate([x1 * cos - x2 * sin, x1 * sin + x2 * cos], axis=-1)


def _attend(q, k, v, scale):
    s = jnp.einsum("bhgqd,bhkd->bhgqk", q, k, preferred_element_type=jnp.float32) * scale
    p = jax.nn.softmax(s, axis=-1).astype(v.dtype)
    return jnp.einsum("bhgqk,bhkv->bhgqv", p, v)


def _block_attend(q, k, v, scale):
    b, hk, g, n, d = q.shape
    nb = n // Q_BLOCK
    qb = jnp.moveaxis(q.reshape(b, hk, g, nb, Q_BLOCK, d), 3, 0)
    ob = lax.map(lambda qi: _attend(qi, k, v, scale), qb)
    return jnp.moveaxis(ob, 0, 3).reshape(b, hk, g, n, v.shape[-1])


def _diff_attention(zl, zc, qk_norm, lam_p, subln, layer_idx, need_ctx):
    lam_init = 0.8 - 0.6 * math.exp(-0.3 * layer_idx)
    lp = lam_p.astype(jnp.float32)
    lam = jnp.exp(jnp.sum(lp[0] * lp[1])) - jnp.exp(jnp.sum(lp[2] * lp[3])) + lam_init
    scale = DA_QK_DIM ** -0.5

    def qkv(z):
        q, k, v = _split_cols(z, (2 * DA_HEADS * DA_QK_DIM, 2 * DA_HEADS * DA_QK_DIM, DA_HEADS * DA_V_DIM))
        q = _rms(_heads(q, 2 * DA_HEADS), qk_norm[0])
        k = _rms(_heads(k, 2 * DA_HEADS), qk_norm[1])
        v = jnp.repeat(_heads(v, DA_HEADS), 2, axis=1)
        return q, k, v

    def combine(o):
        b, _, n, dv = o.shape
        o = o.reshape(b, DA_HEADS, 2, n, dv)
        o = o[:, :, 0] - lam.astype(o.dtype) * o[:, :, 1]
        return _merge(_rms(o, subln) * (1.0 - lam_init))

    cos, sin = _axial_rope(zl.shape[1], DA_QK_DIM, zl.dtype)
    ql, kl, vl = qkv(zl)
    ql, kl = _rope(ql, cos, sin), _rope(kl, cos, sin)
    qc, kc, vc = qkv(zc)
    keys = jnp.concatenate([kc, kl], axis=2)
    vals = jnp.concatenate([vc, vl], axis=2)
    yl = combine(_block_attend(ql[:, :, None], keys, vals, scale)[:, :, 0])
    yc = combine(_attend(qc[:, :, None], kc, vc, scale)[:, :, 0]) if need_ctx else None
    return yl, yc


def _mla(zl, zc, q_norm, kv_norm, w_uq, w_ukv, nope_norm, rope_norm, need_ctx):
    scale = (MLA_NOPE_DIM + MLA_ROPE_DIM) ** -0.5

    def qkv(z, tabs):
        cq, ckv, kr = _split_cols(z, (MLA_Q_RANK, MLA_KV_RANK, MLA_ROPE_DIM))
        q = _heads(_rms(cq, q_norm) @ w_uq, MLA_HEADS)
        kv = _heads(_rms(ckv, kv_norm) @ w_ukv, MLA_HEADS)
        q_nope, q_rope = jnp.split(q, [MLA_NOPE_DIM], axis=-1)
        k_nope, v = jnp.split(kv, [MLA_NOPE_DIM], axis=-1)
        q_nope = _rms(q_nope, nope_norm[0])
        k_nope = _rms(k_nope, nope_norm[1])
        q_rope = _rms(q_rope, rope_norm[0])
        k_rope = _rms(kr[:, None], rope_norm[1])
        if tabs is not None:
            q_rope = _rope(q_rope, *tabs)
            k_rope = _rope(k_rope, *tabs)
        b, h, n, _ = k_nope.shape
        q = jnp.concatenate([q_nope, q_rope], axis=-1)
        k = jnp.concatenate([k_nope, jnp.broadcast_to(k_rope, (b, h, n, MLA_ROPE_DIM))], axis=-1)
        return q, k, v

    ql, kl, vl = qkv(zl, _axial_rope(zl.shape[1], MLA_ROPE_DIM, zl.dtype))
    qc, kc, vc = qkv(zc, None)
    keys = jnp.concatenate([kc, kl], axis=2)
    vals = jnp.concatenate([vc, vl], axis=2)
    yl = _merge(_block_attend(ql[:, :, None], keys, vals, scale)[:, :, 0])
    yc = _merge(_attend(qc[:, :, None], kc, vc, scale)[:, :, 0]) if need_ctx else None
    return yl, yc


def _chunk_scan(q, k, v, logf, s0):
    b, h, n, dk = q.shape
    dv = v.shape[-1]
    nc = n // HG_CHUNK

    def chunks(a):
        return jnp.moveaxis(a.reshape(b, h, nc, HG_CHUNK, a.shape[-1]), 2, 0)

    tri = jnp.tril(jnp.ones((HG_CHUNK, HG_CHUNK), dtype=bool))[:, :, None]

    def step(S, blk):
        qb, kb, vb, gb = blk
        G = jnp.cumsum(gb, axis=2)
        rel = jnp.where(tri, G[:, :, :, None, :] - G[:, :, None, :, :], -jnp.inf)
        att = jnp.einsum("bhtsk,bhsk->bhts", qb[:, :, :, None, :] * jnp.exp(rel), kb)
        o = jnp.einsum("bhts,bhsv->bhtv", att, vb) + jnp.einsum("bhtk,bhkv->bhtv", qb * jnp.exp(G), S)
        G_end = G[:, :, -1:, :]
        S = jnp.exp(G_end[:, :, 0, :, None]) * S + jnp.einsum("bhsk,bhsv->bhkv", kb * jnp.exp(G_end - G), vb)
        return S, o

    S, o = lax.scan(step, s0, (chunks(q), chunks(k), chunks(v), chunks(logf)))
    return jnp.moveaxis(o, 0, 2).reshape(b, h, n, dv), S


def _hgrn2(zl, zc, lb, out_norm, need_ctx):
    def feats(z):
        q, xf, xb, i, g = _split_cols(z, (HG_FDIM, HG_FDIM, HG_FDIM, GROUP_WIDTH, GROUP_WIDTH))
        q = _heads(jax.nn.silu(q), HG_HEADS).astype(jnp.float32)
        v = _heads(i, HG_HEADS).astype(jnp.float32)

        def gate(xd, lbd):
            xd = _heads(xd, HG_HEADS).astype(jnp.float32)
            lbd = lbd.reshape(HG_HEADS, 1, HG_EXPAND)
            logf = jnp.logaddexp(jnp.log(lbd), jnp.log1p(-lbd) + jax.nn.log_sigmoid(xd))
            kk = (1.0 - lbd) * jax.nn.sigmoid(-xd)
            return kk, logf
        return q, v, gate(xf, lb[0]), gate(xb, lb[1]), g

    def flip(a):
        return jnp.flip(a, axis=2)

    def bidir(q, v, fw, bw, s_f, s_b):
        o_f, st_f = _chunk_scan(q, fw[0], v, fw[1], s_f)
        o_b, st_b = _chunk_scan(flip(q), flip(bw[0]), flip(v), flip(bw[1]), s_b)
        return o_f + flip(o_b), st_f, st_b

    def readout(o, g):
        return _merge(_rms(o.astype(g.dtype), out_norm)) * jax.nn.silu(g)

    qc, vc, fwc, bwc, gc = feats(zc)
    b = zc.shape[0]
    s0 = jnp.zeros((b, HG_HEADS, HG_EXPAND, HG_VDIM), jnp.float32)
    oc, sc_f, sc_b = bidir(qc, vc, fwc, bwc, s0, s0)
    ql, vl, fwl, bwl, gl = feats(zl)
    ol, _, _ = bidir(ql, vl, fwl, bwl, sc_f, sc_b)
    yl = readout(ol, gl)
    yc = readout(oc, gc) if need_ctx else None
    return yl, yc


def _gqa(zl, zc, qk_norm, need_ctx):
    scale = HEAD_DIM ** -0.5
    grp = GQ_HEADS // GQ_KV_HEADS

    def qkv(z, tabs):
        q, k, v = _split_cols(z, (GQ_HEADS * HEAD_DIM, GQ_KV_HEADS * HEAD_DIM, GQ_KV_HEADS * HEAD_DIM))
        q = _rms(_heads(q, GQ_HEADS), qk_norm[0])
        k = _rms(_heads(k, GQ_KV_HEADS), qk_norm[1])
        v = _heads(v, GQ_KV_HEADS)
        if tabs is not None:
            q, k = _rope(q, *tabs), _rope(k, *tabs)
        b, _, n, d = q.shape
        return q.reshape(b, GQ_KV_HEADS, grp, n, d), k, v

    def out(o):
        b, hk, g, n, dv = o.shape
        return _merge(o.reshape(b, hk * g, n, dv))

    ql, kl, vl = qkv(zl, _axial_rope(zl.shape[1], HEAD_DIM, zl.dtype))
    qc, kc, vc = qkv(zc, None)
    keys = jnp.concatenate([kc, kl], axis=2)
    vals = jnp.concatenate([vc, vl], axis=2)
    yl = out(_block_attend(ql, keys, vals, scale))
    yc = out(_attend(qc, kc, vc, scale)) if need_ctx else None
    return yl, yc


def _swiglu(h, w_in, w_out):
    g, u = jnp.split(h @ w_in, 2, axis=-1)
    return (jax.nn.silu(g) * u) @ w_out


def _layer(xl, xc, c, c_ctx, layer_idx, last, ada_w, ada_b, norm_w, ffn_w_in, ffn_w_out,
           mix_w_in, mix_w_out, da_qk_norm, da_lambda, da_subln, mla_q_norm, mla_kv_norm,
           mla_w_uq, mla_w_ukv, mla_nope_norm, mla_rope_norm, hg_lb, hg_out_norm, gq_qk_norm):
    need_ctx = not last
    mod = (jax.nn.silu(c) @ ada_w + ada_b).reshape(c.shape[0], N_MOD, 1, D_MODEL)
    mod_c = (jax.nn.silu(c_ctx) @ ada_w + ada_b).reshape(1, N_MOD, 1, D_MODEL)

    def modnorm(x, m, i):
        return _rms(x, norm_w[i]) * (1.0 + m[:, 3 * i + 1]) + m[:, 3 * i]

    xl = xl + HALF * mod[:, 2] * _swiglu(modnorm(xl, mod, 0), ffn_w_in[0], ffn_w_out[0])
    xc = xc + HALF * mod_c[:, 2] * _swiglu(modnorm(xc, mod_c, 0), ffn_w_in[0], ffn_w_out[0])

    zl = modnorm(xl, mod, 1) @ mix_w_in
    zc = modnorm(xc, mod_c, 1) @ mix_w_in
    la, lb, lc, ld = _split_cols(zl, MIX_SPLITS)
    ca, cb, cc, cd = _split_cols(zc, MIX_SPLITS)
    ya_l, ya_c = _diff_attention(la, ca, da_qk_norm, da_lambda, da_subln, layer_idx, need_ctx)
    yb_l, yb_c = _mla(lb, cb, mla_q_norm, mla_kv_norm, mla_w_uq, mla_w_ukv, mla_nope_norm, mla_rope_norm, need_ctx)
    yc_l, yc_c = _hgrn2(lc, cc, hg_lb, hg_out_norm, need_ctx)
    yd_l, yd_c = _gqa(ld, cd, gq_qk_norm, need_ctx)
    xl = xl + mod[:, 5] * (jnp.concatenate([ya_l, yb_l, yc_l, yd_l], axis=-1) @ mix_w_out)
    if need_ctx:
        xc = xc + mod_c[:, 5] * (jnp.concatenate([ya_c, yb_c, yc_c, yd_c], axis=-1) @ mix_w_out)

    xl = xl + HALF * mod[:, 8] * _swiglu(modnorm(xl, mod, 2), ffn_w_in[1], ffn_w_out[1])
    if need_ctx:
        xc = xc + HALF * mod_c[:, 8] * _swiglu(modnorm(xc, mod_c, 2), ffn_w_in[1], ffn_w_out[1])
    return xl, xc


def setup_inputs(seed: int = 0) -> dict:
    key = jax.random.key(seed)
    ks = iter(jax.random.split(key, 32))

    def nrm(shape, s):
        return jax.random.normal(next(ks), shape, jnp.float32) * s

    def gain(shape):
        return 1.0 + nrm(shape, 0.02)

    return {
        "x": nrm((BATCH, SEQ, D_MODEL), 1.0),
        "c": nrm((BATCH, D_MODEL), 1.0),
        "ctx": nrm((BATCH, CTX_LEN, D_MODEL), 1.0),
        "c_ctx": nrm((D_MODEL,), 1.0),
        "ada_w": nrm((DEPTH, D_MODEL, N_MOD * D_MODEL), 0.5 * D_MODEL ** -0.5),
        "ada_b": nrm((DEPTH, N_MOD * D_MODEL), 0.01),
        "norm_w": gain((DEPTH, 3, D_MODEL)),
        "ffn_w_in": nrm((DEPTH, 2, D_MODEL, 2 * D_FF), D_MODEL ** -0.5),
        "ffn_w_out": nrm((DEPTH, 2, D_FF, D_MODEL), D_FF ** -0.5),
        "mix_w_in": nrm((DEPTH, D_MODEL, MIX_COLS), D_MODEL ** -0.5),
        "mix_w_out": nrm((DEPTH, D_MODEL, D_MODEL), D_MODEL ** -0.5),
        "da_qk_norm": gain((DEPTH, 2, DA_QK_DIM)),
        "da_lambda": nrm((DEPTH, 4, DA_QK_DIM), 0.1),
        "da_subln": gain((DEPTH, DA_V_DIM)),
        "mla_q_norm": gain((DEPTH, MLA_Q_RANK)),
        "mla_kv_norm": gain((DEPTH, MLA_KV_RANK)),
        "mla_w_uq": nrm((DEPTH, MLA_Q_RANK, MLA_HEADS * (MLA_NOPE_DIM + MLA_ROPE_DIM)), MLA_Q_RANK ** -0.5),
        "mla_w_ukv": nrm((DEPTH, MLA_KV_RANK, MLA_HEADS * (MLA_NOPE_DIM + MLA_V_DIM)), MLA_KV_RANK ** -0.5),
        "mla_nope_norm": gain((DEPTH, 2, MLA_NOPE_DIM)),
        "mla_rope_norm": gain((DEPTH, 2, MLA_ROPE_DIM)),
        "hg_lb_logits": nrm((2, DEPTH, HG_FDIM), 0.1),
        "hg_out_norm": gain((DEPTH, HG_VDIM)),
        "gq_qk_norm": gain((DEPTH, 2, HEAD_DIM)),
    }


def reference(x, c, ctx, c_ctx, ada_w, ada_b, norm_w, ffn_w_in, ffn_w_out, mix_w_in, mix_w_out,
              da_qk_norm, da_lambda, da_subln, mla_q_norm, mla_kv_norm, mla_w_uq, mla_w_ukv,
              mla_nope_norm, mla_rope_norm, hg_lb_logits, hg_out_norm, gq_qk_norm):
    p = jax.nn.softmax(hg_lb_logits.astype(jnp.float32), axis=1)
    lbs = jnp.maximum(jnp.cumsum(p, axis=1) - p[:, :1], 0.0)
    xl, xc = x, ctx
    for l in range(DEPTH):
        xl, xc = _layer(xl, xc, c, c_ctx, l, l == DEPTH - 1, ada_w[l], ada_b[l], norm_w[l],
                        ffn_w_in[l], ffn_w_out[l], mix_w_in[l], mix_w_out[l], da_qk_norm[l],
                        da_lambda[l], da_subln[l], mla_q_norm[l], mla_kv_norm[l], mla_w_uq[l],
                        mla_w_ukv[l], mla_nope_norm[l], mla_rope_norm[l], lbs[:, l],
                        hg_out_norm[l], gq_qk_norm[l])
    return xl
```

```python
import functools
import math

import numpy as np
import jax
import jax.numpy as jnp
from jax import lax
from jax.experimental import pallas as pl
from jax.experimental.pallas import tpu as pltpu

F32 = jnp.float32
BF16 = jnp.bfloat16

D_MODEL = 1024
HEAD_DIM = 64
GROUP_WIDTH = D_MODEL // 4
EPS = 1e-6
ROPE_THETA = 10000.0
GRID_W = 64
HALF = 0.5
N_MOD = 9
D_FF = 2816

DA_HEADS = 4
DA_QK_DIM = 32
MLA_HEADS = 4
MLA_Q_RANK = 256
MLA_KV_RANK = 128
MLA_NOPE_DIM = 64
MLA_ROPE_DIM = 32
HG_HEADS = 4
HG_EXPAND = 128
HG_FDIM = 512
HG_VDIM = 64
GQ_HEADS = 4
GQ_KV_HEADS = 2

A_COLS = 768
B_COLS = 416
C_COLS = 2048
D_COLS = 512
B_PAD = 512
W_COLS = A_COLS + B_PAD + C_COLS + D_COLS

LANES = 128
ROW_TILE = 256
ATT_TQ = 256
ATT_TK = 512
HG_BLOCK = 256
HG_SUB = 16
FF_CHUNK = 256
VMEM_LIMIT = 56 * 1024 * 1024

P_GAQ, P_GAK, P_QN, P_KVN, P_NOPE0, P_NOPE1, P_ROPE0, P_ROPE1, P_GDQ, P_GDK, P_LEN = (
    0, 256, 512, 768, 896, 1152, 1408, 1536, 1664, 1920, 2048)
O_SUBLN, O_HGN, O_LAM, O_LEN = 0, 512, 768, 896


def _cparams(sem):
    return pltpu.CompilerParams(dimension_semantics=sem, vmem_limit_bytes=VMEM_LIMIT)


def _resident(shape):
    nd = len(shape)
    return pl.BlockSpec(shape, lambda *_: (0,) * nd, pipeline_mode=pl.Buffered(1))


def _dot(a, b):
    return jnp.dot(a, b, preferred_element_type=F32)


def _rms_rows(x):
    return x * lax.rsqrt(jnp.mean(x * x, axis=-1, keepdims=True) + EPS)


def _seg_norm(x, seg, size):
    x2 = x * x
    hi = x2.astype(BF16)
    lo = (x2 - hi.astype(F32)).astype(BF16)
    ss = _dot(hi, seg) + _dot(lo, seg)
    return x * lax.rsqrt(ss * (1.0 / size) + EPS)


def _swap_halves(x, half):
    lane = lax.broadcasted_iota(jnp.int32, x.shape, 1)
    up = pltpu.roll(x, LANES - half, 1)
    down = pltpu.roll(x, half, 1)
    return jnp.where((lane % (2 * half)) < half, up, down)


def _rope(x, cos, sin, half):
    parts = []
    for c in range(x.shape[1] // LANES):
        xs = x[:, c * LANES:(c + 1) * LANES]
        parts.append(xs * cos + _swap_halves(xs, half) * sin)
    return parts[0] if len(parts) == 1 else jnp.concatenate(parts, axis=1)


def _silu(x):
    return x / (1.0 + jnp.exp(-x))


def _mod_kernel(c_ref, w_ref, b_ref, o_ref):
    a = _silu(c_ref[...])
    a_hi = a.astype(BF16)
    a_lo = (a - a_hi.astype(F32)).astype(BF16)
    w = w_ref[...]
    w_hi = w.astype(BF16)
    w_lo = (w - w_hi.astype(F32)).astype(BF16)
    o_ref[...] = _dot(a_hi, w_hi) + _dot(a_hi, w_lo) + _dot(a_lo, w_hi) + b_ref[...]


def _modulation(cc, ada_w, ada_b):
    depth = ada_w.shape[0]
    nb = N_MOD * D_MODEL // D_MODEL
    return pl.pallas_call(
        _mod_kernel,
        grid=(depth, nb),
        in_specs=[
            pl.BlockSpec((8, D_MODEL), lambda l, j: (0, 0)),
            pl.BlockSpec((None, D_MODEL, D_MODEL), lambda l, j: (l, 0, j)),
            pl.BlockSpec((None, 1, D_MODEL), lambda l, j: (l, 0, j)),
        ],
        out_specs=pl.BlockSpec((None, 8, D_MODEL), lambda l, j: (l, 0, j)),
        out_shape=jax.ShapeDtypeStruct((depth, 8, N_MOD * D_MODEL), F32),
        compiler_params=_cparams(("parallel", "parallel")),
        name="adaln_mod",
    )(cc, ada_w, ada_b.reshape(depth, 1, N_MOD * D_MODEL))


def _ffn_kernel(x_ref, mod_ref, nw_ref, win_ref, wout_ref, o_ref, *, sub):
    x = x_ref[...]
    m = mod_ref[...]
    shift, scale, gate = m[3 * sub:3 * sub + 1], m[3 * sub + 1:3 * sub + 2], m[3 * sub + 2:3 * sub + 3]
    h = (_rms_rows(x) * nw_ref[sub:sub + 1]) * (1.0 + scale) + shift
    hb = h.astype(BF16)
    acc = jnp.zeros(x.shape, F32)
    for c in range(D_FF // FF_CHUNK):
        g = _dot(hb, win_ref[:, c * FF_CHUNK:(c + 1) * FF_CHUNK])
        u = _dot(hb, win_ref[:, D_FF + c * FF_CHUNK:D_FF + (c + 1) * FF_CHUNK])
        a = (_silu(g) * u).astype(BF16)
        acc = acc + _dot(a, wout_ref[c * FF_CHUNK:(c + 1) * FF_CHUNK, :])
    o_ref[...] = x + HALF * gate * acc


def _ffn(xs, modsel, nw, win, wout, *, sub, n_lat_tiles, n_tiles):
    bsz = xs.shape[0]
    return pl.pallas_call(
        functools.partial(_ffn_kernel, sub=sub),
        grid=(bsz, n_tiles),
        in_specs=[
            pl.BlockSpec((None, ROW_TILE, D_MODEL), lambda b, i: (b, i, 0)),
            pl.BlockSpec((None, None, N_MOD, D_MODEL),
                         lambda b, i: (b, jnp.where(i >= n_lat_tiles, 1, 0), 0, 0)),
            _resident((3, D_MODEL)),
            _resident((D_MODEL, 2 * D_FF)),
            _resident((D_FF, D_MODEL)),
        ],
        out_specs=pl.BlockSpec((None, ROW_TILE, D_MODEL), lambda b, i: (b, i, 0)),
        out_shape=jax.ShapeDtypeStruct((bsz, n_tiles * ROW_TILE, D_MODEL), F32),
        compiler_params=_cparams(("parallel", "parallel")),
        name=f"ffn_half_{sub}",
    )(xs, modsel, nw, win, wout)


def _prep_kernel(x_ref, mod_ref, nw_ref, w_ref, rope_ref, seg32_ref, seg64_ref, pv_ref, wuq_ref, wukv_ref,
                 qa_ref, ka_ref, va_ref, qb_ref, kb_ref, vb_ref, qd_ref, kd_ref, vd_ref, zc_ref):
    x = x_ref[...]
    m = mod_ref[...]
    h = (_rms_rows(x) * nw_ref[1:2]) * (1.0 + m[4:5]) + m[3:4]
    hb = h.astype(BF16)
    seg32 = seg32_ref[...]
    seg64 = seg64_ref[...]
    cos32, sin32 = rope_ref[:, 0:128], rope_ref[:, 128:256]
    cos64, sin64 = rope_ref[:, 256:384], rope_ref[:, 384:512]

    def pv(lo, hi):
        return pv_ref[:, lo:hi]

    za = _dot(hb, w_ref[:, 0:A_COLS])
    qa = _rope(_seg_norm(za[:, 0:256], seg32, 32) * pv(P_GAQ, P_GAQ + 256), cos32, sin32, 16)
    qa = qa * (DA_QK_DIM ** -0.5)
    ka = _rope(_seg_norm(za[:, 256:512], seg32, 32) * pv(P_GAK, P_GAK + 256), cos32, sin32, 16)
    for hh in range(2 * DA_HEADS):
        qa_ref[hh] = qa[:, 32 * hh:32 * hh + 32].astype(BF16)
        ka_ref[hh] = ka[:, 32 * hh:32 * hh + 32].astype(BF16)
    for hh in range(DA_HEADS):
        va_ref[hh] = za[:, 512 + 64 * hh:512 + 64 * hh + 64].astype(BF16)

    zb = _dot(hb, w_ref[:, A_COLS:A_COLS + B_PAD])
    cq = _rms_rows(zb[:, 0:256]) * pv(P_QN, P_QN + 256)
    q2 = _dot(cq.astype(BF16), wuq_ref[...])
    ckv = _rms_rows(zb[:, 256:384]) * pv(P_KVN, P_KVN + 128)
    kv2 = _dot(ckv.astype(BF16), wukv_ref[...])
    sb = (MLA_NOPE_DIM + MLA_ROPE_DIM) ** -0.5
    qn = _seg_norm(q2[:, 0:256], seg64, 64) * pv(P_NOPE0, P_NOPE0 + 256) * sb
    kn = _seg_norm(kv2[:, 0:256], seg64, 64) * pv(P_NOPE1, P_NOPE1 + 256)
    seg32s = seg32[0:128, 0:128]
    qr = _rope(_seg_norm(q2[:, 256:384], seg32s, 32) * pv(P_ROPE0, P_ROPE0 + 128), cos32, sin32, 16) * sb
    kr = _rope(_seg_norm(zb[:, 384:512], seg32s, 32) * pv(P_ROPE1, P_ROPE1 + 128), cos32, sin32, 16)
    zpad = jnp.zeros((x.shape[0], 32), BF16)
    for hh in range(MLA_HEADS):
        qb_ref[hh, :, 0:64] = qn[:, 64 * hh:64 * hh + 64].astype(BF16)
        qb_ref[hh, :, 64:96] = qr[:, 32 * hh:32 * hh + 32].astype(BF16)
        qb_ref[hh, :, 96:128] = zpad
        kb_ref[hh, :, 0:64] = kn[:, 64 * hh:64 * hh + 64].astype(BF16)
        kb_ref[hh, :, 64:96] = kr[:, 0:32].astype(BF16)
        kb_ref[hh, :, 96:128] = zpad
        vb_ref[hh] = kv2[:, 256 + 64 * hh:256 + 64 * hh + 64].astype(BF16)

    zc_ref[...] = _dot(hb, w_ref[:, A_COLS + B_PAD:A_COLS + B_PAD + C_COLS])

    zd = _dot(hb, w_ref[:, A_COLS + B_PAD + C_COLS:W_COLS])
    qd = _rope(_seg_norm(zd[:, 0:256], seg64, 64) * pv(P_GDQ, P_GDQ + 256), cos64, sin64, 32)
    qd = qd * (HEAD_DIM ** -0.5)
    kd = _rope(_seg_norm(zd[:, 256:384], seg64[0:128, 0:128], 64) * pv(P_GDK, P_GDK + 128), cos64, sin64, 32)
    for hh in range(GQ_HEADS):
        qd_ref[hh] = qd[:, 64 * hh:64 * hh + 64].astype(BF16)
    for hh in range(GQ_KV_HEADS):
        kd_ref[hh] = kd[:, 64 * hh:64 * hh + 64].astype(BF16)
        vd_ref[hh] = zd[:, 384 + 64 * hh:384 + 64 * hh + 64].astype(BF16)


def _prep(xs, modsel, nw, w, rope, seg32, seg64, pvec, wuq, wukv, *, n_lat_tiles):
    bsz, t, _ = xs.shape
    n_tiles = t // ROW_TILE

    def heads(h, d):
        return (pl.BlockSpec((None, h, ROW_TILE, d), lambda b, i: (b, 0, i, 0)),
                jax.ShapeDtypeStruct((bsz, h, t, d), BF16))

    outs = [heads(8, 32), heads(8, 32), heads(4, 64),
            heads(4, 128), heads(4, 128), heads(4, 64),
            heads(4, 64), heads(2, 64), heads(2, 64),
            (pl.BlockSpec((None, ROW_TILE, C_COLS), lambda b, i: (b, i, 0)),
             jax.ShapeDtypeStruct((bsz, t, C_COLS), F32))]
    return pl.pallas_call(
        _prep_kernel,
        grid=(bsz, n_tiles),
        in_specs=[
            pl.BlockSpec((None, ROW_TILE, D_MODEL), lambda b, i: (b, i, 0)),
            pl.BlockSpec((None, None, N_MOD, D_MODEL),
                         lambda b, i: (b, jnp.where(i >= n_lat_tiles, 1, 0), 0, 0)),
            _resident((3, D_MODEL)),
            _resident((D_MODEL, W_COLS)),
            pl.BlockSpec((ROW_TILE, 512), lambda b, i: (i, 0)),
            _resident((256, 256)),
            _resident((256, 256)),
            _resident((1, P_LEN)),
            _resident((MLA_Q_RANK, 384)),
            _resident((MLA_KV_RANK, 512)),
        ],
        out_specs=[o[0] for o in outs],
        out_shape=[o[1] for o in outs],
        compiler_params=_cparams(("parallel", "parallel")),
        name="mix_in_prep",
    )(xs, modsel, nw, w, rope, seg32, seg64, pvec, wuq, wukv)


def _attn_kernel(q_ref, k_ref, v_ref, kc_ref, vc_ref, o_ref, m_scr, l_scr, acc_scr,
                 *, hq, hk, hv, n_lat_q, n_lat_kv):
    i = pl.program_id(1)
    j = pl.program_id(2)

    @pl.when(j == 0)
    def _():
        m_scr[...] = jnp.full(m_scr.shape, -jnp.inf, F32)
        l_scr[...] = jnp.zeros(l_scr.shape, F32)
        acc_scr[...] = jnp.zeros(acc_scr.shape, F32)

    def step(kr, vr):
        for h in range(hq):
            q = q_ref[h]
            k = kr[h // (hq // hk)]
            v = vr[h // (hq // hv)]
            s = lax.dot_general(q, k, (((1,), (1,)), ((), ())), preferred_element_type=F32)
            m_prev = m_scr[h]
            m_new = jnp.maximum(m_prev, jnp.max(s, axis=1, keepdims=True))
            alpha = jnp.exp(m_prev - m_new)
            p = jnp.exp(s - m_new[:, 0:1])
            l_scr[h] = alpha * l_scr[h] + jnp.sum(p, axis=1, keepdims=True)
            m_scr[h] = m_new
            acc_scr[h] = alpha[:, 0:HEAD_DIM] * acc_scr[h] + _dot(p.astype(BF16), v)

    @pl.when(jnp.logical_and(j < n_lat_kv, i < n_lat_q))
    def _():
        step(k_ref, v_ref)

    @pl.when(j == n_lat_kv)
    def _():
        step(kc_ref, vc_ref)
        for h in range(hq):
            o_ref[:, HEAD_DIM * h:HEAD_DIM * (h + 1)] = acc_scr[h] / l_scr[h][:, 0:HEAD_DIM]


def _attention(q, k, v, *, n_lat, n_ctx):
    bsz, hq, t, dq = q.shape
    hk, hv = k.shape[1], v.shape[1]
    n_lat_q = n_lat // ATT_TQ
    n_lat_kv = n_lat // ATT_TK
    ctx_blk = n_lat // n_ctx

    def kv_idx(b, i, j):
        return (b, 0, jnp.where(i >= n_lat_q, 0, jnp.minimum(j, n_lat_kv - 1)), 0)

    return pl.pallas_call(
        functools.partial(_attn_kernel, hq=hq, hk=hk, hv=hv, n_lat_q=n_lat_q, n_lat_kv=n_lat_kv),
        grid=(bsz, t // ATT_TQ, n_lat_kv + 1),
        in_specs=[
            pl.BlockSpec((None, hq, ATT_TQ, dq), lambda b, i, j: (b, 0, i, 0)),
            pl.BlockSpec((None, hk, ATT_TK, dq), kv_idx),
            pl.BlockSpec((None, hv, ATT_TK, HEAD_DIM), kv_idx),
            pl.BlockSpec((None, hk, n_ctx, dq), lambda b, i, j: (b, 0, ctx_blk, 0)),
            pl.BlockSpec((None, hv, n_ctx, HEAD_DIM), lambda b, i, j: (b, 0, ctx_blk, 0)),
        ],
        out_specs=pl.BlockSpec((None, ATT_TQ, hq * HEAD_DIM), lambda b, i, j: (b, i, 0)),
        out_shape=jax.ShapeDtypeStruct((bsz, t, hq * HEAD_DIM), F32),
        scratch_shapes=[
            pltpu.VMEM((hq, ATT_TQ, LANES), F32),
            pltpu.VMEM((hq, ATT_TQ, LANES), F32),
            pltpu.VMEM((hq, ATT_TQ, HEAD_DIM), F32),
        ],
        compiler_params=_cparams(("parallel", "parallel", "arbitrary")),
        name=f"flash_attn_h{hq}_d{dq}",
    )(q, k, v, k, v)


def _hgrn_kernel(zq_ref, zx_ref, zi_ref, lb_ref, tri_ref, ones_ref, o_ref,
                 st_scr, qt_scr, kt_scr, g_scr, q_scr, k_scr, dec_scr):
    d = pl.program_id(1)
    j = pl.program_id(2)

    @pl.when(j == 0)
    def _():
        st_scr[...] = jnp.zeros(st_scr.shape, F32)

    x = zx_ref[...]
    lb = lb_ref[...]
    log_lb, log_1m, one_m = lb[0:1], lb[1:2], lb[2:3]
    e = jnp.exp(-jnp.abs(x))
    log_sig = jnp.minimum(x, 0.0) - jnp.log1p(e)
    b = log_1m + log_sig
    big = jnp.maximum(log_lb, b)
    logf = big + jnp.log1p(jnp.exp(-jnp.abs(log_lb - b)))
    kk = one_m * (1.0 / (1.0 + jnp.exp(x)))
    q = _silu(zq_ref[...])

    h1 = logf.astype(BF16)
    r1 = logf - h1.astype(F32)
    h2 = r1.astype(BF16)
    h3 = (r1 - h2.astype(F32)).astype(BF16)
    tri = tri_ref[...]
    ones = ones_ref[...]
    g = _dot(tri, h1) + _dot(tri, h2) + _dot(tri, h3)
    gtot = _dot(ones, h1) + _dot(ones, h2) + _dot(ones, h3)
    qt_scr[...] = q * jnp.exp(g)
    kt_scr[...] = kk * jnp.exp(gtot - g)
    dec_scr[...] = jnp.exp(gtot)
    g_scr[...] = g
    q_scr[...] = q
    k_scr[...] = kk

    sgn = 1 - 2 * d
    n_sub = HG_BLOCK // HG_SUB
    s_iota = lax.broadcasted_iota(jnp.int32, (HG_SUB, HG_FDIM), 0)

    def body(n, carry):
        ne = jnp.where(d == 0, n, n_sub - 1 - n)
        r0 = pl.multiple_of(ne * HG_SUB, HG_SUB)
        rows = pl.ds(r0, HG_SUB)
        qt_n = qt_scr[rows, :]
        kt_n = kt_scr[rows, :]
        g_n = g_scr[rows, :]
        q_n = q_scr[rows, :]
        k_n = k_scr[rows, :]
        v_n = zi_ref[rows, 0:GROUP_WIDTH]
        dec_n = dec_scr[pl.ds(r0, 1), :]
        st = st_scr[...]
        stb = st.astype(BF16)
        qtb = qt_n.astype(BF16)
        ktb = kt_n.astype(BF16)
        vb = v_n.astype(BF16)

        rows_out = [[] for _ in range(HG_HEADS)]
        for t in range(HG_SUB):
            allowed = ((s_iota - t) * sgn) <= 0
            diff = jnp.where(allowed, g_n[t:t + 1, :] - g_n, -jnp.inf)
            w = jnp.exp(diff) * k_n * q_n[t:t + 1, :]
            for hh in range(HG_HEADS):
                c = jnp.sum(w[:, HG_EXPAND * hh:HG_EXPAND * (hh + 1)], axis=1, keepdims=True)
                rows_out[hh].append(jnp.sum(c * v_n[:, HG_VDIM * hh:HG_VDIM * (hh + 1)], axis=0, keepdims=True))

        outs = []
        upd = []
        for hh in range(HG_HEADS):
            ks = slice(HG_EXPAND * hh, HG_EXPAND * (hh + 1))
            vs = slice(HG_VDIM * hh, HG_VDIM * (hh + 1))
            inter = lax.dot_general(qtb[:, ks], stb[:, ks], (((1,), (1,)), ((), ())), preferred_element_type=F32)
            outs.append(inter + jnp.concatenate(rows_out[hh], axis=0))
            upd.append(lax.dot_general(vb[:, vs], ktb[:, ks], (((0,), (0,)), ((), ())), preferred_element_type=F32))
        o_ref[rows, :] = jnp.concatenate(outs, axis=1)
        st_scr[...] = st * dec_n + jnp.concatenate(upd, axis=1)
        return carry

    lax.fori_loop(0, n_sub, body, 0)


def _hgrn(zc, lbp, tri, ones, *, n_lat, n_ctx):
    bsz, t, _ = zc.shape
    nl = n_lat // HG_BLOCK
    nc = n_ctx // HG_BLOCK

    def blk(d, j):
        ctx = jnp.where(d == 0, nl + j, nl + nc - 1 - j)
        lat = jnp.where(d == 0, j - nc, nl - 1 - (j - nc))
        return jnp.where(j < nc, ctx, lat)

    return pl.pallas_call(
        _hgrn_kernel,
        grid=(bsz, 2, nl + nc),
        in_specs=[
            pl.BlockSpec((None, HG_BLOCK, HG_FDIM), lambda b, d, j: (b, blk(d, j), 0)),
            pl.BlockSpec((None, HG_BLOCK, HG_FDIM), lambda b, d, j: (b, blk(d, j), 1 + d)),
            pl.BlockSpec((None, HG_BLOCK, HG_FDIM), lambda b, d, j: (b, blk(d, j), 3)),
            pl.BlockSpec((None, 8, HG_FDIM), lambda b, d, j: (d, 0, 0)),
            pl.BlockSpec((None, HG_BLOCK, HG_BLOCK), lambda b, d, j: (d, 0, 0)),
            _resident((HG_BLOCK, HG_BLOCK)),
        ],
        out_specs=pl.BlockSpec((None, None, HG_BLOCK, GROUP_WIDTH), lambda b, d, j: (b, d, blk(d, j), 0)),
        out_shape=jax.ShapeDtypeStruct((bsz, 2, t, GROUP_WIDTH), F32),
        scratch_shapes=[pltpu.VMEM((HG_VDIM, HG_FDIM), F32)]
        + [pltpu.VMEM((HG_BLOCK, HG_FDIM), F32) for _ in range(6)],
        compiler_params=_cparams(("parallel", "parallel", "arbitrary")),
        name="hgrn2_scan",
    )(zc, zc, zc, lbp, tri, ones)


def _out_kernel(x_ref, mod_ref, oa_ref, ob_ref, od_ref, ocf_ref, ocb_ref, g_ref, w_ref, pv_ref, seg64_ref, o_ref):
    x = x_ref[...]
    m = mod_ref[...]
    seg64 = seg64_ref[...]
    lam = pv_ref[:, O_LAM:O_LAM + LANES]

    oa = oa_ref[...]
    lane = lax.broadcasted_iota(jnp.int32, (x.shape[0], LANES), 1)
    ya = []
    for c in range(DA_HEADS):
        pair = oa[:, c * LANES:(c + 1) * LANES]
        diff = pair - lam * pltpu.roll(pair, HEAD_DIM, 1)
        ya.append(jnp.where(lane < HEAD_DIM, diff, 0.0))
    ya = jnp.concatenate(ya, axis=1)
    ya = jnp.concatenate([_seg_norm(ya[:, 0:256], seg64, 64), _seg_norm(ya[:, 256:512], seg64, 64)], axis=1)
    ya = ya * pv_ref[:, O_SUBLN:O_SUBLN + 512]

    oc = ocf_ref[...] + ocb_ref[...]
    yc = _seg_norm(oc, seg64, 64) * pv_ref[:, O_HGN:O_HGN + 256] * _silu(g_ref[...])

    acc = _dot(ya.astype(BF16), w_ref[0:512, :])
    acc = acc + _dot(ob_ref[...].astype(BF16), w_ref[512:768, :])
    acc = acc + _dot(yc.astype(BF16), w_ref[768:1024, :])
    acc = acc + _dot(od_ref[...].astype(BF16), w_ref[1024:1280, :])
    o_ref[...] = x + m[5:6] * acc


def _out_proj(xs, modsel, oa, ob, od, oc, zc, w, pvec, seg64, *, n_lat_tiles):
    bsz, t, _ = xs.shape
    n_tiles = t // ROW_TILE
    row = lambda width: pl.BlockSpec((None, ROW_TILE, width), lambda b, i: (b, i, 0))
    return pl.pallas_call(
        _out_kernel,
        grid=(bsz, n_tiles),
        in_specs=[
            row(D_MODEL),
            pl.BlockSpec((None, None, N_MOD, D_MODEL),
                         lambda b, i: (b, jnp.where(i >= n_lat_tiles, 1, 0), 0, 0)),
            row(512), row(256), row(256),
            pl.BlockSpec((None, None, ROW_TILE, GROUP_WIDTH), lambda b, i: (b, 0, i, 0)),
            pl.BlockSpec((None, None, ROW_TILE, GROUP_WIDTH), lambda b, i: (b, 1, i, 0)),
            pl.BlockSpec((None, ROW_TILE, GROUP_WIDTH), lambda b, i: (b, i, C_COLS // GROUP_WIDTH - 1)),
            _resident((1280, D_MODEL)),
            _resident((1, O_LEN)),
            _resident((256, 256)),
        ],
        out_specs=row(D_MODEL),
        out_shape=jax.ShapeDtypeStruct((bsz, t, D_MODEL), F32),
        compiler_params=_cparams(("parallel", "parallel")),
        name="mix_out_proj",
    )(xs, modsel, oa, ob, od, oc, oc, zc, w, pvec, seg64)


def _block_diag_ones(n, size):
    idx = np.arange(n) // size
    return jnp.asarray((idx[:, None] == idx[None, :]).astype(np.float32), dtype=BF16)


def _rope_table(n_lat, n_ctx):
    rows = n_lat // GRID_W
    r = jnp.repeat(jnp.arange(rows), GRID_W).astype(F32)
    col = jnp.tile(jnp.arange(GRID_W), rows).astype(F32)

    def pattern(rot_dim):
        n_freq = rot_dim // 4
        inv = ROPE_THETA ** (-jnp.arange(n_freq, dtype=F32) / n_freq)
        ang = jnp.concatenate([r[:, None] * inv, col[:, None] * inv], axis=-1)
        cos, sin = jnp.cos(ang), jnp.sin(ang)
        cos_l = jnp.tile(jnp.concatenate([cos, cos], axis=-1), (1, LANES // rot_dim))
        sin_l = jnp.tile(jnp.concatenate([-sin, sin], axis=-1), (1, LANES // rot_dim))
        cos_l = jnp.concatenate([cos_l, jnp.ones((n_ctx, LANES), F32)], axis=0)
        sin_l = jnp.concatenate([sin_l, jnp.zeros((n_ctx, LANES), F32)], axis=0)
        return cos_l, sin_l

    c32, s32 = pattern(32)
    c64, s64 = pattern(64)
    return jnp.concatenate([c32, s32, c64, s64], axis=1)


def _tile_lanes(v, reps):
    return jnp.tile(v.astype(F32), reps)


def kernel(x, c, ctx, c_ctx, ada_w, ada_b, norm_w, ffn_w_in, ffn_w_out, mix_w_in, mix_w_out, da_qk_norm, da_lambda,
           da_subln, mla_q_norm, mla_kv_norm, mla_w_uq, mla_w_ukv, mla_nope_norm, mla_rope_norm, hg_lb_logits,
           hg_out_norm, gq_qk_norm):
    bsz, n_lat, _ = x.shape
    n_ctx = ctx.shape[1]
    depth = ada_w.shape[0]
    assert bsz + 1 <= 8 and n_lat % ATT_TK == 0 and n_lat % n_ctx == 0
    assert n_ctx % ROW_TILE == 0 and n_ctx % HG_BLOCK == 0 and n_ctx % ATT_TQ == 0
    t = n_lat + n_ctx
    n_lat_tiles = n_lat // ROW_TILE
    n_tiles = t // ROW_TILE

    xs = jnp.concatenate([x, ctx], axis=1)

    cc = jnp.zeros((8, D_MODEL), F32).at[0:bsz].set(c).at[bsz].set(c_ctx)
    mods = _modulation(cc, ada_w, ada_b).reshape(depth, 8, N_MOD, D_MODEL)

    p = jax.nn.softmax(hg_lb_logits.astype(F32), axis=1)
    lbs = jnp.maximum(jnp.cumsum(p, axis=1) - p[:, :1], 0.0)

    rope = _rope_table(n_lat, n_ctx)
    seg32 = _block_diag_ones(256, 32)
    seg64 = _block_diag_ones(256, 64)
    ones_sub = _block_diag_ones(HG_BLOCK, HG_SUB)
    ii = np.arange(HG_BLOCK)
    same = (ii[:, None] // HG_SUB) == (ii[None, :] // HG_SUB)
    tri = jnp.asarray(np.stack([same & (ii[None, :] <= ii[:, None]), same & (ii[None, :] >= ii[:, None])])
                      .astype(np.float32), dtype=BF16)

    qd_ = MLA_NOPE_DIM + MLA_ROPE_DIM
    uq_perm = np.concatenate([np.arange(h * qd_, h * qd_ + MLA_NOPE_DIM) for h in range(MLA_HEADS)]
                             + [np.arange(h * qd_ + MLA_NOPE_DIM, (h + 1) * qd_) for h in range(MLA_HEADS)])
    kd_ = MLA_NOPE_DIM + HEAD_DIM
    ukv_perm = np.concatenate([np.arange(h * kd_, h * kd_ + MLA_NOPE_DIM) for h in range(MLA_HEADS)]
                              + [np.arange(h * kd_ + MLA_NOPE_DIM, (h + 1) * kd_) for h in range(MLA_HEADS)])

    for l in range(depth):
        modsel = jnp.stack([mods[l, 0:bsz], jnp.broadcast_to(mods[l, bsz], (bsz, N_MOD, D_MODEL))], axis=1)
        nw = norm_w[l].astype(F32)
        last = l == depth - 1

        xs = _ffn(xs, modsel, nw, ffn_w_in[l, 0].astype(BF16), ffn_w_out[l, 0].astype(BF16),
                  sub=0, n_lat_tiles=n_lat_tiles, n_tiles=n_tiles)

        wl = mix_w_in[l]
        w_in = jnp.concatenate(
            [wl[:, 0:A_COLS], wl[:, A_COLS:A_COLS + B_COLS], jnp.zeros((D_MODEL, B_PAD - B_COLS), F32),
             wl[:, A_COLS + B_COLS:]], axis=1).astype(BF16)
        pvec = jnp.concatenate([
            _tile_lanes(da_qk_norm[l, 0], 8), _tile_lanes(da_qk_norm[l, 1], 8),
            mla_q_norm[l].astype(F32), mla_kv_norm[l].astype(F32),
            _tile_lanes(mla_nope_norm[l, 0], 4), _tile_lanes(mla_nope_norm[l, 1], 4),
            _tile_lanes(mla_rope_norm[l, 0], 4), _tile_lanes(mla_rope_norm[l, 1], 4),
            _tile_lanes(gq_qk_norm[l, 0], 4), _tile_lanes(gq_qk_norm[l, 1], 2)]).reshape(1, P_LEN)
        qa, ka, va, qb, kb, vb, qd, kd, vd, zc = _prep(
            xs, modsel, nw, w_in, rope, seg32, seg64, pvec,
            mla_w_uq[l][:, uq_perm].astype(BF16), mla_w_ukv[l][:, ukv_perm].astype(BF16),
            n_lat_tiles=n_lat_tiles)

        oa = _attention(qa, ka, va, n_lat=n_lat, n_ctx=n_ctx)
        ob = _attention(qb, kb, vb, n_lat=n_lat, n_ctx=n_ctx)
        od = _attention(qd, kd, vd, n_lat=n_lat, n_ctx=n_ctx)

        lb = lbs[:, l]
        lbp = jnp.zeros((2, 8, HG_FDIM), F32)
        lbp = lbp.at[:, 0].set(jnp.log(lb)).at[:, 1].set(jnp.log1p(-lb)).at[:, 2].set(1.0 - lb)
        oc = _hgrn(zc, lbp, tri, ones_sub, n_lat=n_lat, n_ctx=n_ctx)

        lam_init = 0.8 - 0.6 * math.exp(-0.3 * l)
        lp = da_lambda[l].astype(F32)
        lam = jnp.exp(jnp.sum(lp[0] * lp[1])) - jnp.exp(jnp.sum(lp[2] * lp[3])) + lam_init
        subln = jnp.concatenate([da_subln[l].astype(F32) * (1.0 - lam_init), jnp.zeros((HEAD_DIM,), F32)])
        opv = jnp.concatenate([jnp.tile(subln, DA_HEADS), _tile_lanes(hg_out_norm[l], HG_HEADS),
                               jnp.full((LANES,), lam, F32)]).reshape(1, O_LEN)
        wo = mix_w_out[l]
        wa = jnp.concatenate([wo[0:GROUP_WIDTH].reshape(DA_HEADS, HEAD_DIM, D_MODEL),
                              jnp.zeros((DA_HEADS, HEAD_DIM, D_MODEL), F32)], axis=1).reshape(2 * GROUP_WIDTH, D_MODEL)
        w_out = jnp.concatenate([wa, wo[GROUP_WIDTH:]], axis=0).astype(BF16)
        xs = _out_proj(xs, modsel, oa, ob, od, oc, zc, w_out, opv, seg64, n_lat_tiles=n_lat_tiles)

        xs = _ffn(xs, modsel, nw, ffn_w_in[l, 1].astype(BF16), ffn_w_out[l, 1].astype(BF16),
                  sub=2, n_lat_tiles=n_lat_tiles, n_tiles=n_lat_tiles if last else n_tiles)
    return xs
```

```python
import functools
import math

import numpy as np
import jax
import jax.numpy as jnp
from jax import lax
from jax.experimental import pallas as pl
from jax.experimental.pallas import tpu as pltpu

F32 = jnp.float32
BF16 = jnp.bfloat16

D_MODEL = 1024
HEAD_DIM = 64
GROUP_WIDTH = D_MODEL // 4
EPS = 1e-6
ROPE_THETA = 10000.0
GRID_W = 64
HALF = 0.5
N_MOD = 9
D_FF = 2816

DA_HEADS = 4
DA_QK_DIM = 32
MLA_HEADS = 4
MLA_Q_RANK = 256
MLA_KV_RANK = 128
MLA_NOPE_DIM = 64
MLA_ROPE_DIM = 32
HG_HEADS = 4
HG_EXPAND = 128
HG_FDIM = 512
HG_VDIM = 64
GQ_HEADS = 4
GQ_KV_HEADS = 2

A_COLS = 768
B_COLS = 416
C_COLS = 2048
D_COLS = 512
B_PAD = 512
W_COLS = A_COLS + B_PAD + C_COLS + D_COLS

LANES = 128
ROW_TILE = 256
ATT_TQ = 512
ATT_TK = 1024
LOG2E = 1.4426950408889634
FAST_SCORE_BOUND = 64.0
HG_BLOCK = 256
HG_SUB = 16
FF_CHUNK = 256
VMEM_LIMIT = 56 * 1024 * 1024

P_GAQ, P_GAK, P_QN, P_KVN, P_NOPE0, P_NOPE1, P_ROPE0, P_ROPE1, P_GDQ, P_GDK, P_LEN = (
    0, 256, 512, 768, 896, 1152, 1408, 1536, 1664, 1920, 2048)
O_SUBLN, O_HGN, O_LAM, O_LEN = 0, 512, 768, 896


def _cparams(sem):
    return pltpu.CompilerParams(dimension_semantics=sem, vmem_limit_bytes=VMEM_LIMIT)


def _resident(shape):
    nd = len(shape)
    return pl.BlockSpec(shape, lambda *_: (0,) * nd, pipeline_mode=pl.Buffered(1))


def _dot(a, b):
    return jnp.dot(a, b, preferred_element_type=F32)


def _rms_rows(x):
    return x * lax.rsqrt(jnp.mean(x * x, axis=-1, keepdims=True) + EPS)


def _seg_norm(x, seg, size):
    x2 = x * x
    hi = x2.astype(BF16)
    lo = (x2 - hi.astype(F32)).astype(BF16)
    ss = _dot(hi, seg) + _dot(lo, seg)
    return x * lax.rsqrt(ss * (1.0 / size) + EPS)


def _swap_halves(x, half):
    lane = lax.broadcasted_iota(jnp.int32, x.shape, 1)
    up = pltpu.roll(x, LANES - half, 1)
    down = pltpu.roll(x, half, 1)
    return jnp.where((lane % (2 * half)) < half, up, down)


def _rope(x, cos, sin, half):
    parts = []
    for c in range(x.shape[1] // LANES):
        xs = x[:, c * LANES:(c + 1) * LANES]
        parts.append(xs * cos + _swap_halves(xs, half) * sin)
    return parts[0] if len(parts) == 1 else jnp.concatenate(parts, axis=1)


def _silu(x):
    return x / (1.0 + jnp.exp(-x))


def _mod_kernel(c_ref, w_ref, b_ref, o_ref):
    a = _silu(c_ref[...])
    a_hi = a.astype(BF16)
    a_lo = (a - a_hi.astype(F32)).astype(BF16)
    w = w_ref[...]
    w_hi = w.astype(BF16)
    w_lo = (w - w_hi.astype(F32)).astype(BF16)
    o_ref[...] = _dot(a_hi, w_hi) + _dot(a_hi, w_lo) + _dot(a_lo, w_hi) + b_ref[...]


def _modulation(cc, ada_w, ada_b):
    depth = ada_w.shape[0]
    nb = N_MOD * D_MODEL // D_MODEL
    return pl.pallas_call(
        _mod_kernel,
        grid=(depth, nb),
        in_specs=[
            pl.BlockSpec((8, D_MODEL), lambda l, j: (0, 0)),
            pl.BlockSpec((None, D_MODEL, D_MODEL), lambda l, j: (l, 0, j)),
            pl.BlockSpec((None, 1, D_MODEL), lambda l, j: (l, 0, j)),
        ],
        out_specs=pl.BlockSpec((None, 8, D_MODEL), lambda l, j: (l, 0, j)),
        out_shape=jax.ShapeDtypeStruct((depth, 8, N_MOD * D_MODEL), F32),
        compiler_params=_cparams(("parallel", "parallel")),
        name="adaln_mod",
    )(cc, ada_w, ada_b.reshape(depth, 1, N_MOD * D_MODEL))


def _ffn_kernel(x_ref, mod_ref, nw_ref, win_ref, wout_ref, o_ref, *, sub):
    x = x_ref[...]
    m = mod_ref[...]
    shift, scale, gate = m[3 * sub:3 * sub + 1], m[3 * sub + 1:3 * sub + 2], m[3 * sub + 2:3 * sub + 3]
    h = (_rms_rows(x) * nw_ref[sub:sub + 1]) * (1.0 + scale) + shift
    hb = h.astype(BF16)
    acc = jnp.zeros(x.shape, F32)
    for c in range(D_FF // FF_CHUNK):
        g = _dot(hb, win_ref[:, c * FF_CHUNK:(c + 1) * FF_CHUNK])
        u = _dot(hb, win_ref[:, D_FF + c * FF_CHUNK:D_FF + (c + 1) * FF_CHUNK])
        a = (_silu(g) * u).astype(BF16)
        acc = acc + _dot(a, wout_ref[c * FF_CHUNK:(c + 1) * FF_CHUNK, :])
    o_ref[...] = x + HALF * gate * acc


def _ffn(xs, modsel, nw, win, wout, *, sub, n_lat_tiles, n_tiles):
    bsz = xs.shape[0]
    return pl.pallas_call(
        functools.partial(_ffn_kernel, sub=sub),
        grid=(bsz, n_tiles),
        in_specs=[
            pl.BlockSpec((None, ROW_TILE, D_MODEL), lambda b, i: (b, i, 0)),
            pl.BlockSpec((None, None, N_MOD, D_MODEL),
                         lambda b, i: (b, jnp.where(i >= n_lat_tiles, 1, 0), 0, 0)),
            _resident((3, D_MODEL)),
            _resident((D_MODEL, 2 * D_FF)),
            _resident((D_FF, D_MODEL)),
        ],
        out_specs=pl.BlockSpec((None, ROW_TILE, D_MODEL), lambda b, i: (b, i, 0)),
        out_shape=jax.ShapeDtypeStruct((bsz, n_tiles * ROW_TILE, D_MODEL), F32),
        compiler_params=_cparams(("parallel", "parallel")),
        name=f"ffn_half_{sub}",
    )(xs, modsel, nw, win, wout)


def _prep_kernel(x_ref, mod_ref, nw_ref, w_ref, rope_ref, seg32_ref, seg64_ref, pv_ref, wuq_ref, wukv_ref,
                 qa_ref, ka_ref, va_ref, qb_ref, kb_ref, vb_ref, qd_ref, kd_ref, vd_ref, zc_ref):
    x = x_ref[...]
    m = mod_ref[...]
    h = (_rms_rows(x) * nw_ref[1:2]) * (1.0 + m[4:5]) + m[3:4]
    hb = h.astype(BF16)
    seg32 = seg32_ref[...]
    seg64 = seg64_ref[...]
    cos32, sin32 = rope_ref[:, 0:128], rope_ref[:, 128:256]
    cos64, sin64 = rope_ref[:, 256:384], rope_ref[:, 384:512]

    def pv(lo, hi):
        return pv_ref[:, lo:hi]

    za = _dot(hb, w_ref[:, 0:A_COLS])
    qa = _rope(_seg_norm(za[:, 0:256], seg32, 32) * pv(P_GAQ, P_GAQ + 256), cos32, sin32, 16)
    qa = qa * (DA_QK_DIM ** -0.5 * LOG2E)
    ka = _rope(_seg_norm(za[:, 256:512], seg32, 32) * pv(P_GAK, P_GAK + 256), cos32, sin32, 16)
    for hh in range(2 * DA_HEADS):
        qa_ref[hh] = qa[:, 32 * hh:32 * hh + 32].astype(BF16)
        ka_ref[hh] = ka[:, 32 * hh:32 * hh + 32].astype(BF16)
    ones_col = jnp.where(lax.broadcasted_iota(jnp.int32, (x.shape[0], HEAD_DIM), 1) == 0, 1.0, 0.0).astype(BF16)

    def put_v(ref, hh, val):
        ref[hh, :, 0:HEAD_DIM] = val.astype(BF16)
        ref[hh, :, HEAD_DIM:LANES] = ones_col

    for hh in range(DA_HEADS):
        put_v(va_ref, hh, za[:, 512 + 64 * hh:512 + 64 * hh + 64])

    zb = _dot(hb, w_ref[:, A_COLS:A_COLS + B_PAD])
    cq = _rms_rows(zb[:, 0:256]) * pv(P_QN, P_QN + 256)
    q2 = _dot(cq.astype(BF16), wuq_ref[...])
    ckv = _rms_rows(zb[:, 256:384]) * pv(P_KVN, P_KVN + 128)
    kv2 = _dot(ckv.astype(BF16), wukv_ref[...])
    sb = (MLA_NOPE_DIM + MLA_ROPE_DIM) ** -0.5 * LOG2E
    qn = _seg_norm(q2[:, 0:256], seg64, 64) * pv(P_NOPE0, P_NOPE0 + 256) * sb
    kn = _seg_norm(kv2[:, 0:256], seg64, 64) * pv(P_NOPE1, P_NOPE1 + 256)
    seg32s = seg32[0:128, 0:128]
    qr = _rope(_seg_norm(q2[:, 256:384], seg32s, 32) * pv(P_ROPE0, P_ROPE0 + 128), cos32, sin32, 16) * sb
    kr = _rope(_seg_norm(zb[:, 384:512], seg32s, 32) * pv(P_ROPE1, P_ROPE1 + 128), cos32, sin32, 16)
    zpad = jnp.zeros((x.shape[0], 32), BF16)
    for hh in range(MLA_HEADS):
        qb_ref[hh, :, 0:64] = qn[:, 64 * hh:64 * hh + 64].astype(BF16)
        qb_ref[hh, :, 64:96] = qr[:, 32 * hh:32 * hh + 32].astype(BF16)
        qb_ref[hh, :, 96:128] = zpad
        kb_ref[hh, :, 0:64] = kn[:, 64 * hh:64 * hh + 64].astype(BF16)
        kb_ref[hh, :, 64:96] = kr[:, 0:32].astype(BF16)
        kb_ref[hh, :, 96:128] = zpad
        put_v(vb_ref, hh, kv2[:, 256 + 64 * hh:256 + 64 * hh + 64])

    zc_ref[...] = _dot(hb, w_ref[:, A_COLS + B_PAD:A_COLS + B_PAD + C_COLS])

    zd = _dot(hb, w_ref[:, A_COLS + B_PAD + C_COLS:W_COLS])
    qd = _rope(_seg_norm(zd[:, 0:256], seg64, 64) * pv(P_GDQ, P_GDQ + 256), cos64, sin64, 32)
    qd = qd * (HEAD_DIM ** -0.5 * LOG2E)
    kd = _rope(_seg_norm(zd[:, 256:384], seg64[0:128, 0:128], 64) * pv(P_GDK, P_GDK + 128), cos64, sin64, 32)
    for hh in range(GQ_HEADS):
        qd_ref[hh] = qd[:, 64 * hh:64 * hh + 64].astype(BF16)
    for hh in range(GQ_KV_HEADS):
        kd_ref[hh] = kd[:, 64 * hh:64 * hh + 64].astype(BF16)
        put_v(vd_ref, hh, zd[:, 384 + 64 * hh:384 + 64 * hh + 64])


def _prep(xs, modsel, nw, w, rope, seg32, seg64, pvec, wuq, wukv, *, n_lat_tiles):
    bsz, t, _ = xs.shape
    n_tiles = t // ROW_TILE

    def heads(h, d):
        return (pl.BlockSpec((None, h, ROW_TILE, d), lambda b, i: (b, 0, i, 0)),
                jax.ShapeDtypeStruct((bsz, h, t, d), BF16))

    outs = [heads(8, 32), heads(8, 32), heads(4, LANES),
            heads(4, 128), heads(4, 128), heads(4, LANES),
            heads(4, 64), heads(2, 64), heads(2, LANES),
            (pl.BlockSpec((None, ROW_TILE, C_COLS), lambda b, i: (b, i, 0)),
             jax.ShapeDtypeStruct((bsz, t, C_COLS), F32))]
    return pl.pallas_call(
        _prep_kernel,
        grid=(bsz, n_tiles),
        in_specs=[
            pl.BlockSpec((None, ROW_TILE, D_MODEL), lambda b, i: (b, i, 0)),
            pl.BlockSpec((None, None, N_MOD, D_MODEL),
                         lambda b, i: (b, jnp.where(i >= n_lat_tiles, 1, 0), 0, 0)),
            _resident((3, D_MODEL)),
            _resident((D_MODEL, W_COLS)),
            pl.BlockSpec((ROW_TILE, 512), lambda b, i: (i, 0)),
            _resident((256, 256)),
            _resident((256, 256)),
            _resident((1, P_LEN)),
            _resident((MLA_Q_RANK, 384)),
            _resident((MLA_KV_RANK, 512)),
        ],
        out_specs=[o[0] for o in outs],
        out_shape=[o[1] for o in outs],
        compiler_params=_cparams(("parallel", "parallel")),
        name="mix_in_prep",
    )(xs, modsel, nw, w, rope, seg32, seg64, pvec, wuq, wukv)


def _attn_kernel(flag_ref, q_ref, k_ref, v_ref, kc_ref, vc_ref, o_ref, acc_scr, m_scr, *, hq, hk, hv, n_kv):
    j = pl.program_id(2)
    fast = flag_ref[0] == 1

    @pl.when(j == 0)
    def _():
        acc_scr[...] = jnp.zeros(acc_scr.shape, F32)
        m_scr[...] = jnp.full(m_scr.shape, -jnp.inf, F32)

    def scores(h, kr):
        return lax.dot_general(q_ref[h], kr[h // (hq // hk)], (((1,), (1,)), ((), ())),
                               preferred_element_type=F32)

    def step_fast(kr, vr):
        for h in range(hq):
            p = jnp.exp2(scores(h, kr)).astype(BF16)
            acc_scr[h] += _dot(p, vr[h // (hq // hv)])

    def step_safe(kr, vr):
        for h in range(hq):
            s = scores(h, kr)
            m_prev = m_scr[h]
            m_new = jnp.maximum(m_prev, jnp.max(s, axis=1, keepdims=True))
            alpha = jnp.exp2(m_prev - m_new)
            p = jnp.exp2(s - m_new[:, 0:1]).astype(BF16)
            acc_scr[h] = alpha * acc_scr[h] + _dot(p, vr[h // (hq // hv)])
            m_scr[h] = m_new

    def both(kr, vr):
        @pl.when(fast)
        def _():
            step_fast(kr, vr)

        @pl.when(jnp.logical_not(fast))
        def _():
            step_safe(kr, vr)

    @pl.when(j < n_kv)
    def _():
        both(k_ref, v_ref)

    @pl.when(j == n_kv)
    def _():
        both(kc_ref, vc_ref)
        for h in range(hq):
            a = acc_scr[h]
            o_ref[:, HEAD_DIM * h:HEAD_DIM * (h + 1)] = a[:, 0:HEAD_DIM] / a[:, HEAD_DIM:HEAD_DIM + 1]


def _attention(flag, q, k, v, prev, *, n_lat, n_ctx, latent):
    bsz, hq, t, dq = q.shape
    hk, hv = k.shape[1], v.shape[1]
    width = hq * HEAD_DIM
    ctx_blk = n_lat // n_ctx
    if latent:
        tq, nq, n_kv, q_off = ATT_TQ, n_lat // ATT_TQ, n_lat // ATT_TK, 0
    else:
        tq, nq, n_kv, q_off = n_ctx, 1, 0, ctx_blk

    def kv_idx(b, i, j, f):
        return (b, 0, jnp.minimum(j, max(n_kv - 1, 0)), 0)

    in_specs = [
        pl.BlockSpec((None, hq, tq, dq), lambda b, i, j, f: (b, 0, i + q_off, 0)),
        pl.BlockSpec((None, hk, ATT_TK, dq), kv_idx),
        pl.BlockSpec((None, hv, ATT_TK, LANES), kv_idx),
        pl.BlockSpec((None, hk, n_ctx, dq), lambda b, i, j, f: (b, 0, ctx_blk, 0)),
        pl.BlockSpec((None, hv, n_ctx, LANES), lambda b, i, j, f: (b, 0, ctx_blk, 0)),
    ]
    args = [flag, q, k, v, k, v]
    aliases = {}
    if not latent:
        in_specs.append(pl.BlockSpec(memory_space=pl.ANY))
        args.append(prev)
        aliases = {6: 0}

    def body(flag_ref, q_ref, k_ref, v_ref, kc_ref, vc_ref, *rest):
        o_ref, acc_scr, m_scr = rest[-3:]
        _attn_kernel(flag_ref, q_ref, k_ref, v_ref, kc_ref, vc_ref, o_ref, acc_scr, m_scr,
                     hq=hq, hk=hk, hv=hv, n_kv=n_kv)

    grid_spec = pltpu.PrefetchScalarGridSpec(
        num_scalar_prefetch=1,
        grid=(bsz, nq, n_kv + 1),
        in_specs=in_specs,
        out_specs=pl.BlockSpec((None, tq, width), lambda b, i, j, f: (b, i + q_off, 0)),
        scratch_shapes=[pltpu.VMEM((hq, tq, LANES), F32), pltpu.VMEM((hq, tq, LANES), F32)],
    )
    return pl.pallas_call(
        body,
        grid_spec=grid_spec,
        out_shape=jax.ShapeDtypeStruct((bsz, t, width), F32),
        input_output_aliases=aliases,
        compiler_params=_cparams(("parallel", "parallel", "arbitrary")),
        name=f"flash_attn_h{hq}_d{dq}_{'lat' if latent else 'ctx'}",
    )(*args)


def _attend(flag, q, k, v, *, n_lat, n_ctx, need_ctx):
    o = _attention(flag, q, k, v, None, n_lat=n_lat, n_ctx=n_ctx, latent=True)
    if need_ctx:
        o = _attention(flag, q, k, v, o, n_lat=n_lat, n_ctx=n_ctx, latent=False)
    return o


def _hgrn_kernel(zq_ref, zx_ref, zi_ref, lb_ref, tri_ref, ones_ref, o_ref,
                 st_scr, qt_scr, kt_scr, g_scr, q_scr, k_scr, dec_scr):
    d = pl.program_id(1)
    j = pl.program_id(2)

    @pl.when(j == 0)
    def _():
        st_scr[...] = jnp.zeros(st_scr.shape, F32)

    x = zx_ref[...]
    lb = lb_ref[...]
    log_lb, log_1m, one_m = lb[0:1], lb[1:2], lb[2:3]
    e = jnp.exp(-jnp.abs(x))
    log_sig = jnp.minimum(x, 0.0) - jnp.log1p(e)
    b = log_1m + log_sig
    big = jnp.maximum(log_lb, b)
    logf = big + jnp.log1p(jnp.exp(-jnp.abs(log_lb - b)))
    kk = one_m * (1.0 / (1.0 + jnp.exp(x)))
    q = _silu(zq_ref[...])

    h1 = logf.astype(BF16)
    r1 = logf - h1.astype(F32)
    h2 = r1.astype(BF16)
    h3 = (r1 - h2.astype(F32)).astype(BF16)
    tri = tri_ref[...]
    ones = ones_ref[...]
    g = _dot(tri, h1) + _dot(tri, h2) + _dot(tri, h3)
    gtot = _dot(ones, h1) + _dot(ones, h2) + _dot(ones, h3)
    qt_scr[...] = q * jnp.exp(g)
    kt_scr[...] = kk * jnp.exp(gtot - g)
    dec_scr[...] = jnp.exp(gtot)
    g_scr[...] = g
    q_scr[...] = q
    k_scr[...] = kk

    sgn = 1 - 2 * d
    n_sub = HG_BLOCK // HG_SUB
    s_iota = lax.broadcasted_iota(jnp.int32, (HG_SUB, HG_FDIM), 0)

    def body(n, carry):
        ne = jnp.where(d == 0, n, n_sub - 1 - n)
        r0 = pl.multiple_of(ne * HG_SUB, HG_SUB)
        rows = pl.ds(r0, HG_SUB)
        qt_n = qt_scr[rows, :]
        kt_n = kt_scr[rows, :]
        g_n = g_scr[rows, :]
        q_n = q_scr[rows, :]
        k_n = k_scr[rows, :]
        v_n = zi_ref[rows, 0:GROUP_WIDTH]
        dec_n = dec_scr[pl.ds(r0, 1), :]
        st = st_scr[...]
        stb = st.astype(BF16)
        qtb = qt_n.astype(BF16)
        ktb = kt_n.astype(BF16)
        vb = v_n.astype(BF16)

        rows_out = [[] for _ in range(HG_HEADS)]
        for t in range(HG_SUB):
            allowed = ((s_iota - t) * sgn) <= 0
            diff = jnp.where(allowed, g_n[t:t + 1, :] - g_n, -jnp.inf)
            w = jnp.exp(diff) * k_n * q_n[t:t + 1, :]
            for hh in range(HG_HEADS):
                c = jnp.sum(w[:, HG_EXPAND * hh:HG_EXPAND * (hh + 1)], axis=1, keepdims=True)
                rows_out[hh].append(jnp.sum(c * v_n[:, HG_VDIM * hh:HG_VDIM * (hh + 1)], axis=0, keepdims=True))

        outs = []
        upd = []
        for hh in range(HG_HEADS):
            ks = slice(HG_EXPAND * hh, HG_EXPAND * (hh + 1))
            vs = slice(HG_VDIM * hh, HG_VDIM * (hh + 1))
            inter = lax.dot_general(qtb[:, ks], stb[:, ks], (((1,), (1,)), ((), ())), preferred_element_type=F32)
            outs.append(inter + jnp.concatenate(rows_out[hh], axis=0))
            upd.append(lax.dot_general(vb[:, vs], ktb[:, ks], (((0,), (0,)), ((), ())), preferred_element_type=F32))
        o_ref[rows, :] = jnp.concatenate(outs, axis=1)
        st_scr[...] = st * dec_n + jnp.concatenate(upd, axis=1)
        return carry

    lax.fori_loop(0, n_sub, body, 0)


def _hgrn(zc, lbp, tri, ones, *, n_lat, n_ctx):
    bsz, t, _ = zc.shape
    nl = n_lat // HG_BLOCK
    nc = n_ctx // HG_BLOCK

    def blk(d, j):
        ctx = jnp.where(d == 0, nl + j, nl + nc - 1 - j)
        lat = jnp.where(d == 0, j - nc, nl - 1 - (j - nc))
        return jnp.where(j < nc, ctx, lat)

    return pl.pallas_call(
        _hgrn_kernel,
        grid=(bsz, 2, nl + nc),
        in_specs=[
            pl.BlockSpec((None, HG_BLOCK, HG_FDIM), lambda b, d, j: (b, blk(d, j), 0)),
            pl.BlockSpec((None, HG_BLOCK, HG_FDIM), lambda b, d, j: (b, blk(d, j), 1 + d)),
            pl.BlockSpec((None, HG_BLOCK, HG_FDIM), lambda b, d, j: (b, blk(d, j), 3)),
            pl.BlockSpec((None, 8, HG_FDIM), lambda b, d, j: (d, 0, 0)),
            pl.BlockSpec((None, HG_BLOCK, HG_BLOCK), lambda b, d, j: (d, 0, 0)),
            _resident((HG_BLOCK, HG_BLOCK)),
        ],
        out_specs=pl.BlockSpec((None, None, HG_BLOCK, GROUP_WIDTH), lambda b, d, j: (b, d, blk(d, j), 0)),
        out_shape=jax.ShapeDtypeStruct((bsz, 2, t, GROUP_WIDTH), F32),
        scratch_shapes=[pltpu.VMEM((HG_VDIM, HG_FDIM), F32)]
        + [pltpu.VMEM((HG_BLOCK, HG_FDIM), F32) for _ in range(6)],
        compiler_params=_cparams(("parallel", "parallel", "arbitrary")),
        name="hgrn2_scan",
    )(zc, zc, zc, lbp, tri, ones)


def _out_kernel(x_ref, mod_ref, oa_ref, ob_ref, od_ref, ocf_ref, ocb_ref, g_ref, w_ref, pv_ref, seg64_ref, o_ref):
    x = x_ref[...]
    m = mod_ref[...]
    seg64 = seg64_ref[...]
    lam = pv_ref[:, O_LAM:O_LAM + LANES]

    oa = oa_ref[...]
    lane = lax.broadcasted_iota(jnp.int32, (x.shape[0], LANES), 1)
    ya = []
    for c in range(DA_HEADS):
        pair = oa[:, c * LANES:(c + 1) * LANES]
        diff = pair - lam * pltpu.roll(pair, HEAD_DIM, 1)
        ya.append(jnp.where(lane < HEAD_DIM, diff, 0.0))
    ya = jnp.concatenate(ya, axis=1)
    ya = jnp.concatenate([_seg_norm(ya[:, 0:256], seg64, 64), _seg_norm(ya[:, 256:512], seg64, 64)], axis=1)
    ya = ya * pv_ref[:, O_SUBLN:O_SUBLN + 512]

    oc = ocf_ref[...] + ocb_ref[...]
    yc = _seg_norm(oc, seg64, 64) * pv_ref[:, O_HGN:O_HGN + 256] * _silu(g_ref[...])

    acc = _dot(ya.astype(BF16), w_ref[0:512, :])
    acc = acc + _dot(ob_ref[...].astype(BF16), w_ref[512:768, :])
    acc = acc + _dot(yc.astype(BF16), w_ref[768:1024, :])
    acc = acc + _dot(od_ref[...].astype(BF16), w_ref[1024:1280, :])
    o_ref[...] = x + m[5:6] * acc


def _out_proj(xs, modsel, oa, ob, od, oc, zc, w, pvec, seg64, *, n_lat_tiles, n_tiles):
    bsz, t, _ = xs.shape
    row = lambda width: pl.BlockSpec((None, ROW_TILE, width), lambda b, i: (b, i, 0))
    return pl.pallas_call(
        _out_kernel,
        grid=(bsz, n_tiles),
        in_specs=[
            row(D_MODEL),
            pl.BlockSpec((None, None, N_MOD, D_MODEL),
                         lambda b, i: (b, jnp.where(i >= n_lat_tiles, 1, 0), 0, 0)),
            row(512), row(256), row(256),
            pl.BlockSpec((None, None, ROW_TILE, GROUP_WIDTH), lambda b, i: (b, 0, i, 0)),
            pl.BlockSpec((None, None, ROW_TILE, GROUP_WIDTH), lambda b, i: (b, 1, i, 0)),
            pl.BlockSpec((None, ROW_TILE, GROUP_WIDTH), lambda b, i: (b, i, C_COLS // GROUP_WIDTH - 1)),
            _resident((1280, D_MODEL)),
            _resident((1, O_LEN)),
            _resident((256, 256)),
        ],
        out_specs=row(D_MODEL),
        out_shape=jax.ShapeDtypeStruct((bsz, t, D_MODEL), F32),
        compiler_params=_cparams(("parallel", "parallel")),
        name="mix_out_proj",
    )(xs, modsel, oa, ob, od, oc, oc, zc, w, pvec, seg64)


def _block_diag_ones(n, size):
    idx = np.arange(n) // size
    return jnp.asarray((idx[:, None] == idx[None, :]).astype(np.float32), dtype=BF16)


def _rope_table(n_lat, n_ctx):
    rows = n_lat // GRID_W
    r = jnp.repeat(jnp.arange(rows), GRID_W).astype(F32)
    col = jnp.tile(jnp.arange(GRID_W), rows).astype(F32)

    def pattern(rot_dim):
        n_freq = rot_dim // 4
        inv = ROPE_THETA ** (-jnp.arange(n_freq, dtype=F32) / n_freq)
        ang = jnp.concatenate([r[:, None] * inv, col[:, None] * inv], axis=-1)
        cos, sin = jnp.cos(ang), jnp.sin(ang)
        cos_l = jnp.tile(jnp.concatenate([cos, cos], axis=-1), (1, LANES // rot_dim))
        sin_l = jnp.tile(jnp.concatenate([-sin, sin], axis=-1), (1, LANES // rot_dim))
        cos_l = jnp.concatenate([cos_l, jnp.ones((n_ctx, LANES), F32)], axis=0)
        sin_l = jnp.concatenate([sin_l, jnp.zeros((n_ctx, LANES), F32)], axis=0)
        return cos_l, sin_l

    c32, s32 = pattern(32)
    c64, s64 = pattern(64)
    return jnp.concatenate([c32, s32, c64, s64], axis=1)


def _tile_lanes(v, reps):
    return jnp.tile(v.astype(F32), reps)


def kernel(x, c, ctx, c_ctx, ada_w, ada_b, norm_w, ffn_w_in, ffn_w_out, mix_w_in, mix_w_out, da_qk_norm, da_lambda,
           da_subln, mla_q_norm, mla_kv_norm, mla_w_uq, mla_w_ukv, mla_nope_norm, mla_rope_norm, hg_lb_logits,
           hg_out_norm, gq_qk_norm):
    bsz, n_lat, _ = x.shape
    n_ctx = ctx.shape[1]
    depth = ada_w.shape[0]
    assert bsz + 1 <= 8 and n_lat % ATT_TK == 0 and n_lat % ATT_TQ == 0 and n_lat % n_ctx == 0
    assert n_ctx % ROW_TILE == 0 and n_ctx % HG_BLOCK == 0
    t = n_lat + n_ctx
    n_lat_tiles = n_lat // ROW_TILE
    n_tiles = t // ROW_TILE

    xs = jnp.concatenate([x, ctx], axis=1)

    cc = jnp.zeros((8, D_MODEL), F32).at[0:bsz].set(c).at[bsz].set(c_ctx)
    mods = _modulation(cc, ada_w, ada_b).reshape(depth, 8, N_MOD, D_MODEL)

    p = jax.nn.softmax(hg_lb_logits.astype(F32), axis=1)
    lbs = jnp.maximum(jnp.cumsum(p, axis=1) - p[:, :1], 0.0)

    rope = _rope_table(n_lat, n_ctx)
    seg32 = _block_diag_ones(256, 32)
    seg64 = _block_diag_ones(256, 64)
    ones_sub = _block_diag_ones(HG_BLOCK, HG_SUB)
    ii = np.arange(HG_BLOCK)
    same = (ii[:, None] // HG_SUB) == (ii[None, :] // HG_SUB)
    tri = jnp.asarray(np.stack([same & (ii[None, :] <= ii[:, None]), same & (ii[None, :] >= ii[:, None])])
                      .astype(np.float32), dtype=BF16)

    def group_cols(w, first):
        r = w.shape[0]
        w3 = w.reshape(r, MLA_HEADS, -1)
        return jnp.concatenate([w3[:, :, :first].reshape(r, -1), w3[:, :, first:].reshape(r, -1)], axis=1)

    def fast_flag(qn2, kn2, scale):
        bound = jnp.sqrt(qn2 * kn2) * (scale * LOG2E * 1.02)
        return (bound <= FAST_SCORE_BOUND).astype(jnp.int32).reshape(1)

    def gmax2(g):
        return jnp.max(jnp.abs(g.astype(F32))) ** 2

    for l in range(depth):
        modsel = jnp.stack([mods[l, 0:bsz], jnp.broadcast_to(mods[l, bsz], (bsz, N_MOD, D_MODEL))], axis=1)
        nw = norm_w[l].astype(F32)
        last = l == depth - 1

        xs = _ffn(xs, modsel, nw, ffn_w_in[l, 0].astype(BF16), ffn_w_out[l, 0].astype(BF16),
                  sub=0, n_lat_tiles=n_lat_tiles, n_tiles=n_tiles)

        wl = mix_w_in[l]
        w_in = jnp.concatenate(
            [wl[:, 0:A_COLS], wl[:, A_COLS:A_COLS + B_COLS], jnp.zeros((D_MODEL, B_PAD - B_COLS), F32),
             wl[:, A_COLS + B_COLS:]], axis=1).astype(BF16)
        pvec = jnp.concatenate([
            _tile_lanes(da_qk_norm[l, 0], 8), _tile_lanes(da_qk_norm[l, 1], 8),
            mla_q_norm[l].astype(F32), mla_kv_norm[l].astype(F32),
            _tile_lanes(mla_nope_norm[l, 0], 4), _tile_lanes(mla_nope_norm[l, 1], 4),
            _tile_lanes(mla_rope_norm[l, 0], 4), _tile_lanes(mla_rope_norm[l, 1], 4),
            _tile_lanes(gq_qk_norm[l, 0], 4), _tile_lanes(gq_qk_norm[l, 1], 2)]).reshape(1, P_LEN)
        qa, ka, va, qb, kb, vb, qd, kd, vd, zc = _prep(
            xs, modsel, nw, w_in, rope, seg32, seg64, pvec,
            group_cols(mla_w_uq[l], MLA_NOPE_DIM).astype(BF16), group_cols(mla_w_ukv[l], MLA_NOPE_DIM).astype(BF16),
            n_lat_tiles=n_lat_tiles)

        fa = fast_flag(DA_QK_DIM * gmax2(da_qk_norm[l, 0]), DA_QK_DIM * gmax2(da_qk_norm[l, 1]), DA_QK_DIM ** -0.5)
        fb = fast_flag(MLA_NOPE_DIM * gmax2(mla_nope_norm[l, 0]) + MLA_ROPE_DIM * gmax2(mla_rope_norm[l, 0]),
                       MLA_NOPE_DIM * gmax2(mla_nope_norm[l, 1]) + MLA_ROPE_DIM * gmax2(mla_rope_norm[l, 1]),
                       (MLA_NOPE_DIM + MLA_ROPE_DIM) ** -0.5)
        fd = fast_flag(HEAD_DIM * gmax2(gq_qk_norm[l, 0]), HEAD_DIM * gmax2(gq_qk_norm[l, 1]), HEAD_DIM ** -0.5)
        oa = _attend(fa, qa, ka, va, n_lat=n_lat, n_ctx=n_ctx, need_ctx=not last)
        ob = _attend(fb, qb, kb, vb, n_lat=n_lat, n_ctx=n_ctx, need_ctx=not last)
        od = _attend(fd, qd, kd, vd, n_lat=n_lat, n_ctx=n_ctx, need_ctx=not last)

        lb = lbs[:, l]
        lbp = jnp.zeros((2, 8, HG_FDIM), F32)
        lbp = lbp.at[:, 0].set(jnp.log(lb)).at[:, 1].set(jnp.log1p(-lb)).at[:, 2].set(1.0 - lb)
        oc = _hgrn(zc, lbp, tri, ones_sub, n_lat=n_lat, n_ctx=n_ctx)

        lam_init = 0.8 - 0.6 * math.exp(-0.3 * l)
        lp = da_lambda[l].astype(F32)
        lam = jnp.exp(jnp.sum(lp[0] * lp[1])) - jnp.exp(jnp.sum(lp[2] * lp[3])) + lam_init
        subln = jnp.concatenate([da_subln[l].astype(F32) * (1.0 - lam_init), jnp.zeros((HEAD_DIM,), F32)])
        opv = jnp.concatenate([jnp.tile(subln, DA_HEADS), _tile_lanes(hg_out_norm[l], HG_HEADS),
                               jnp.full((LANES,), lam, F32)]).reshape(1, O_LEN)
        wo = mix_w_out[l]
        wa = jnp.concatenate([wo[0:GROUP_WIDTH].reshape(DA_HEADS, HEAD_DIM, D_MODEL),
                              jnp.zeros((DA_HEADS, HEAD_DIM, D_MODEL), F32)], axis=1).reshape(2 * GROUP_WIDTH, D_MODEL)
        w_out = jnp.concatenate([wa, wo[GROUP_WIDTH:]], axis=0).astype(BF16)
        xs = _out_proj(xs, modsel, oa, ob, od, oc, zc, w_out, opv, seg64, n_lat_tiles=n_lat_tiles,
                       n_tiles=n_lat_tiles if last else n_tiles)

        xs = _ffn(xs, modsel, nw, ffn_w_in[l, 1].astype(BF16), ffn_w_out[l, 1].astype(BF16),
                  sub=2, n_lat_tiles=n_lat_tiles, n_tiles=n_lat_tiles if last else n_tiles)
    return xs
```

```python
import functools
import math

import numpy as np
import jax
import jax.numpy as jnp
from jax import lax
from jax.experimental import pallas as pl
from jax.experimental.pallas import tpu as pltpu

F32 = jnp.float32
BF16 = jnp.bfloat16

D_MODEL = 1024
HEAD_DIM = 64
GROUP_WIDTH = D_MODEL // 4
EPS = 1e-6
ROPE_THETA = 10000.0
GRID_W = 64
HALF = 0.5
N_MOD = 9
D_FF = 2816

DA_HEADS = 4
DA_QK_DIM = 32
MLA_HEADS = 4
MLA_Q_RANK = 256
MLA_KV_RANK = 128
MLA_NOPE_DIM = 64
MLA_ROPE_DIM = 32
HG_HEADS = 4
HG_EXPAND = 128
HG_FDIM = 512
HG_VDIM = 64
GQ_HEADS = 4
GQ_KV_HEADS = 2

A_COLS = 768
B_COLS = 416
C_COLS = 2048
D_COLS = 512
B_PAD = 512
W_COLS = A_COLS + B_PAD + C_COLS + D_COLS

LANES = 128
ROW_TILE = 256
ATT_TQ = 512
ATT_TK = 1024
V_ROWS = 80
LOG2E = 1.4426950408889634
FAST_SCORE_BOUND = 64.0
HG_BLOCK = 256
HG_SUB = 16
FF_CHUNK = 256
VMEM_LIMIT = 56 * 1024 * 1024

P_GAQ, P_GAK, P_QN, P_KVN, P_NOPE0, P_NOPE1, P_ROPE0, P_ROPE1, P_GDQ, P_GDK, P_LEN = (
    0, 256, 512, 768, 896, 1152, 1408, 1536, 1664, 1920, 2048)
O_SUBLN, O_HGN, O_LAM, O_LEN = 0, 512, 768, 896


def _cparams(sem):
    return pltpu.CompilerParams(dimension_semantics=sem, vmem_limit_bytes=VMEM_LIMIT)


def _resident(shape):
    nd = len(shape)
    return pl.BlockSpec(shape, lambda *_: (0,) * nd, pipeline_mode=pl.Buffered(1))


def _dot(a, b):
    return jnp.dot(a, b, preferred_element_type=F32)


def _rms_rows(x):
    return x * lax.rsqrt(jnp.mean(x * x, axis=-1, keepdims=True) + EPS)


def _seg_norm(x, seg, size):
    x2 = x * x
    hi = x2.astype(BF16)
    lo = (x2 - hi.astype(F32)).astype(BF16)
    ss = _dot(hi, seg) + _dot(lo, seg)
    return x * lax.rsqrt(ss * (1.0 / size) + EPS)


def _swap_halves(x, half):
    lane = lax.broadcasted_iota(jnp.int32, x.shape, 1)
    up = pltpu.roll(x, LANES - half, 1)
    down = pltpu.roll(x, half, 1)
    return jnp.where((lane % (2 * half)) < half, up, down)


def _rope(x, cos, sin, half):
    parts = []
    for c in range(x.shape[1] // LANES):
        xs = x[:, c * LANES:(c + 1) * LANES]
        parts.append(xs * cos + _swap_halves(xs, half) * sin)
    return parts[0] if len(parts) == 1 else jnp.concatenate(parts, axis=1)


def _silu(x):
    return x / (1.0 + jnp.exp(-x))


def _mod_kernel(c_ref, w_ref, b_ref, o_ref):
    a = _silu(c_ref[...])
    a_hi = a.astype(BF16)
    a_lo = (a - a_hi.astype(F32)).astype(BF16)
    w = w_ref[...]
    w_hi = w.astype(BF16)
    w_lo = (w - w_hi.astype(F32)).astype(BF16)
    o_ref[...] = _dot(a_hi, w_hi) + _dot(a_hi, w_lo) + _dot(a_lo, w_hi) + b_ref[...]


def _modulation(cc, ada_w, ada_b):
    depth = ada_w.shape[0]
    nb = N_MOD * D_MODEL // D_MODEL
    return pl.pallas_call(
        _mod_kernel,
        grid=(depth, nb),
        in_specs=[
            pl.BlockSpec((8, D_MODEL), lambda l, j: (0, 0)),
            pl.BlockSpec((None, D_MODEL, D_MODEL), lambda l, j: (l, 0, j)),
            pl.BlockSpec((None, 1, D_MODEL), lambda l, j: (l, 0, j)),
        ],
        out_specs=pl.BlockSpec((None, 8, D_MODEL), lambda l, j: (l, 0, j)),
        out_shape=jax.ShapeDtypeStruct((depth, 8, N_MOD * D_MODEL), F32),
        compiler_params=_cparams(("parallel", "parallel")),
        name="adaln_mod",
    )(cc, ada_w, ada_b.reshape(depth, 1, N_MOD * D_MODEL))


def _ffn_kernel(x_ref, mod_ref, nw_ref, win_ref, wout_ref, o_ref, *, sub):
    x = x_ref[...]
    m = mod_ref[...]
    shift, scale, gate = m[3 * sub:3 * sub + 1], m[3 * sub + 1:3 * sub + 2], m[3 * sub + 2:3 * sub + 3]
    h = (_rms_rows(x) * nw_ref[sub:sub + 1]) * (1.0 + scale) + shift
    hb = h.astype(BF16)
    acc = jnp.zeros(x.shape, F32)
    for c in range(D_FF // FF_CHUNK):
        g = _dot(hb, win_ref[:, c * FF_CHUNK:(c + 1) * FF_CHUNK])
        u = _dot(hb, win_ref[:, D_FF + c * FF_CHUNK:D_FF + (c + 1) * FF_CHUNK])
        a = (_silu(g) * u).astype(BF16)
        acc = acc + _dot(a, wout_ref[c * FF_CHUNK:(c + 1) * FF_CHUNK, :])
    o_ref[...] = x + HALF * gate * acc


def _ffn(xs, modsel, nw, win, wout, *, sub, n_lat_tiles, n_tiles):
    bsz = xs.shape[0]
    return pl.pallas_call(
        functools.partial(_ffn_kernel, sub=sub),
        grid=(bsz, n_tiles),
        in_specs=[
            pl.BlockSpec((None, ROW_TILE, D_MODEL), lambda b, i: (b, i, 0)),
            pl.BlockSpec((None, None, N_MOD, D_MODEL),
                         lambda b, i: (b, jnp.where(i >= n_lat_tiles, 1, 0), 0, 0)),
            _resident((3, D_MODEL)),
            _resident((D_MODEL, 2 * D_FF)),
            _resident((D_FF, D_MODEL)),
        ],
        out_specs=pl.BlockSpec((None, ROW_TILE, D_MODEL), lambda b, i: (b, i, 0)),
        out_shape=jax.ShapeDtypeStruct((bsz, n_tiles * ROW_TILE, D_MODEL), F32),
        compiler_params=_cparams(("parallel", "parallel")),
        name=f"ffn_half_{sub}",
    )(xs, modsel, nw, win, wout)


def _prep_kernel(x_ref, mod_ref, nw_ref, w_ref, rope_ref, seg32_ref, seg64_ref, pv_ref, wuq_ref, wukv_ref,
                 qa_ref, ka_ref, va_ref, qb_ref, kb_ref, vb_ref, qd_ref, kd_ref, vd_ref, zc_ref):
    x = x_ref[...]
    m = mod_ref[...]
    h = (_rms_rows(x) * nw_ref[1:2]) * (1.0 + m[4:5]) + m[3:4]
    hb = h.astype(BF16)
    seg32 = seg32_ref[...]
    seg64 = seg64_ref[...]
    cos32, sin32 = rope_ref[:, 0:128], rope_ref[:, 128:256]
    cos64, sin64 = rope_ref[:, 256:384], rope_ref[:, 384:512]

    def pv(lo, hi):
        return pv_ref[:, lo:hi]

    za = _dot(hb, w_ref[:, 0:A_COLS])
    qa = _rope(_seg_norm(za[:, 0:256], seg32, 32) * pv(P_GAQ, P_GAQ + 256), cos32, sin32, 16)
    qa = qa * (DA_QK_DIM ** -0.5 * LOG2E)
    ka = _rope(_seg_norm(za[:, 256:512], seg32, 32) * pv(P_GAK, P_GAK + 256), cos32, sin32, 16)
    qa_t = jnp.transpose(qa)
    for hh in range(2 * DA_HEADS):
        qa_ref[hh] = qa_t[32 * hh:32 * hh + 32, :].astype(BF16)
        ka_ref[hh] = ka[:, 32 * hh:32 * hh + 32].astype(BF16)
    ones_row = jnp.where(lax.broadcasted_iota(jnp.int32, (V_ROWS - HEAD_DIM, x.shape[0]), 0) == 0, 1.0, 0.0)
    ones_row = ones_row.astype(BF16)

    def put_v(ref, hh, val_t):
        ref[hh, 0:HEAD_DIM, :] = val_t.astype(BF16)
        ref[hh, HEAD_DIM:V_ROWS, :] = ones_row

    va_t = jnp.transpose(za[:, 512:768])
    for hh in range(DA_HEADS):
        put_v(va_ref, hh, va_t[64 * hh:64 * hh + 64, :])

    zb = _dot(hb, w_ref[:, A_COLS:A_COLS + B_PAD])
    cq = _rms_rows(zb[:, 0:256]) * pv(P_QN, P_QN + 256)
    q2 = _dot(cq.astype(BF16), wuq_ref[...])
    ckv = _rms_rows(zb[:, 256:384]) * pv(P_KVN, P_KVN + 128)
    kv2 = _dot(ckv.astype(BF16), wukv_ref[...])
    sb = (MLA_NOPE_DIM + MLA_ROPE_DIM) ** -0.5 * LOG2E
    qn = _seg_norm(q2[:, 0:256], seg64, 64) * pv(P_NOPE0, P_NOPE0 + 256) * sb
    kn = _seg_norm(kv2[:, 0:256], seg64, 64) * pv(P_NOPE1, P_NOPE1 + 256)
    seg32s = seg32[0:128, 0:128]
    qr = _rope(_seg_norm(q2[:, 256:384], seg32s, 32) * pv(P_ROPE0, P_ROPE0 + 128), cos32, sin32, 16) * sb
    kr = _rope(_seg_norm(zb[:, 384:512], seg32s, 32) * pv(P_ROPE1, P_ROPE1 + 128), cos32, sin32, 16)
    zpad = jnp.zeros((x.shape[0], 32), BF16)
    zpad_t = jnp.zeros((32, x.shape[0]), BF16)
    qn_t = jnp.transpose(qn)
    qr_t = jnp.transpose(qr)
    vb_t = jnp.transpose(kv2[:, 256:512])
    for hh in range(MLA_HEADS):
        qb_ref[hh, 0:64, :] = qn_t[64 * hh:64 * hh + 64, :].astype(BF16)
        qb_ref[hh, 64:96, :] = qr_t[32 * hh:32 * hh + 32, :].astype(BF16)
        qb_ref[hh, 96:128, :] = zpad_t
        kb_ref[hh, :, 0:64] = kn[:, 64 * hh:64 * hh + 64].astype(BF16)
        kb_ref[hh, :, 64:96] = kr[:, 0:32].astype(BF16)
        kb_ref[hh, :, 96:128] = zpad
        put_v(vb_ref, hh, vb_t[64 * hh:64 * hh + 64, :])

    zc_ref[...] = _dot(hb, w_ref[:, A_COLS + B_PAD:A_COLS + B_PAD + C_COLS])

    zd = _dot(hb, w_ref[:, A_COLS + B_PAD + C_COLS:W_COLS])
    qd = _rope(_seg_norm(zd[:, 0:256], seg64, 64) * pv(P_GDQ, P_GDQ + 256), cos64, sin64, 32)
    qd = qd * (HEAD_DIM ** -0.5 * LOG2E)
    kd = _rope(_seg_norm(zd[:, 256:384], seg64[0:128, 0:128], 64) * pv(P_GDK, P_GDK + 128), cos64, sin64, 32)
    qd_t = jnp.transpose(qd)
    vd_t = jnp.transpose(zd[:, 384:512])
    for hh in range(GQ_HEADS):
        qd_ref[hh] = qd_t[64 * hh:64 * hh + 64, :].astype(BF16)
    for hh in range(GQ_KV_HEADS):
        kd_ref[hh] = kd[:, 64 * hh:64 * hh + 64].astype(BF16)
        put_v(vd_ref, hh, vd_t[64 * hh:64 * hh + 64, :])


def _prep(xs, modsel, nw, w, rope, seg32, seg64, pvec, wuq, wukv, *, n_lat_tiles):
    bsz, t, _ = xs.shape
    n_tiles = t // ROW_TILE

    def heads(h, d):
        return (pl.BlockSpec((None, h, ROW_TILE, d), lambda b, i: (b, 0, i, 0)),
                jax.ShapeDtypeStruct((bsz, h, t, d), BF16))

    def heads_t(h, d):
        return (pl.BlockSpec((None, h, d, ROW_TILE), lambda b, i: (b, 0, 0, i)),
                jax.ShapeDtypeStruct((bsz, h, d, t), BF16))

    outs = [heads_t(8, 32), heads(8, 32), heads_t(4, V_ROWS),
            heads_t(4, 128), heads(4, 128), heads_t(4, V_ROWS),
            heads_t(4, 64), heads(2, 64), heads_t(2, V_ROWS),
            (pl.BlockSpec((None, ROW_TILE, C_COLS), lambda b, i: (b, i, 0)),
             jax.ShapeDtypeStruct((bsz, t, C_COLS), F32))]
    return pl.pallas_call(
        _prep_kernel,
        grid=(bsz, n_tiles),
        in_specs=[
            pl.BlockSpec((None, ROW_TILE, D_MODEL), lambda b, i: (b, i, 0)),
            pl.BlockSpec((None, None, N_MOD, D_MODEL),
                         lambda b, i: (b, jnp.where(i >= n_lat_tiles, 1, 0), 0, 0)),
            _resident((3, D_MODEL)),
            _resident((D_MODEL, W_COLS)),
            pl.BlockSpec((ROW_TILE, 512), lambda b, i: (i, 0)),
            _resident((256, 256)),
            _resident((256, 256)),
            _resident((1, P_LEN)),
            _resident((MLA_Q_RANK, 384)),
            _resident((MLA_KV_RANK, 512)),
        ],
        out_specs=[o[0] for o in outs],
        out_shape=[o[1] for o in outs],
        compiler_params=_cparams(("parallel", "parallel")),
        name="mix_in_prep",
    )(xs, modsel, nw, w, rope, seg32, seg64, pvec, wuq, wukv)


def _attn_kernel(flag_ref, q_ref, k_ref, v_ref, kc_ref, vc_ref, o_ref, acc_scr, m_scr, *, hq, hk, hv, n_kv):
    j = pl.program_id(2)
    fast = flag_ref[0] == 1

    @pl.when(j == 0)
    def _():
        acc_scr[...] = jnp.zeros(acc_scr.shape, F32)
        m_scr[...] = jnp.full(m_scr.shape, -jnp.inf, F32)

    def scores(h, kr):
        return _dot(kr[h // (hq // hk)], q_ref[h])

    def step_fast(kr, vr):
        for h in range(hq):
            p = jnp.exp2(scores(h, kr)).astype(BF16)
            acc_scr[h] += _dot(vr[h // (hq // hv)], p)

    def step_safe(kr, vr):
        for h in range(hq):
            s = scores(h, kr)
            m_prev = m_scr[h]
            m_new = jnp.maximum(m_prev, jnp.max(s, axis=0, keepdims=True))
            alpha = jnp.exp2(m_prev - m_new)
            p = jnp.exp2(s - m_new[0:1, :]).astype(BF16)
            acc_scr[h] = alpha[0:1, :] * acc_scr[h] + _dot(vr[h // (hq // hv)], p)
            m_scr[h] = m_new

    def both(kr, vr):
        @pl.when(fast)
        def _():
            step_fast(kr, vr)

        @pl.when(jnp.logical_not(fast))
        def _():
            step_safe(kr, vr)

    @pl.when(j < n_kv)
    def _():
        both(k_ref, v_ref)

    @pl.when(j == n_kv)
    def _():
        both(kc_ref, vc_ref)
        outs = []
        for h in range(hq):
            a = acc_scr[h]
            outs.append(a[0:HEAD_DIM, :] / a[HEAD_DIM:HEAD_DIM + 1, :])
        o_ref[...] = jnp.transpose(jnp.concatenate(outs, axis=0))


def _attention(flag, q, k, v, prev, *, n_lat, n_ctx, latent):
    bsz, hq, dq, t = q.shape
    hk, hv = k.shape[1], v.shape[1]
    width = hq * HEAD_DIM
    ctx_blk = n_lat // n_ctx
    if latent:
        tq, nq, n_kv, q_off = ATT_TQ, n_lat // ATT_TQ, n_lat // ATT_TK, 0
    else:
        tq, nq, n_kv, q_off = n_ctx, 1, 0, ctx_blk

    def kv_blk(j):
        return jnp.minimum(j, max(n_kv - 1, 0))

    in_specs = [
        pl.BlockSpec((None, hq, dq, tq), lambda b, i, j, f: (b, 0, 0, i + q_off)),
        pl.BlockSpec((None, hk, ATT_TK, dq), lambda b, i, j, f: (b, 0, kv_blk(j), 0)),
        pl.BlockSpec((None, hv, V_ROWS, ATT_TK), lambda b, i, j, f: (b, 0, 0, kv_blk(j))),
        pl.BlockSpec((None, hk, n_ctx, dq), lambda b, i, j, f: (b, 0, ctx_blk, 0)),
        pl.BlockSpec((None, hv, V_ROWS, n_ctx), lambda b, i, j, f: (b, 0, 0, ctx_blk)),
    ]
    args = [flag, q, k, v, k, v]
    aliases = {}
    if not latent:
        in_specs.append(pl.BlockSpec(memory_space=pl.ANY))
        args.append(prev)
        aliases = {6: 0}

    def body(flag_ref, q_ref, k_ref, v_ref, kc_ref, vc_ref, *rest):
        o_ref, acc_scr, m_scr = rest[-3:]
        _attn_kernel(flag_ref, q_ref, k_ref, v_ref, kc_ref, vc_ref, o_ref, acc_scr, m_scr,
                     hq=hq, hk=hk, hv=hv, n_kv=n_kv)

    grid_spec = pltpu.PrefetchScalarGridSpec(
        num_scalar_prefetch=1,
        grid=(bsz, nq, n_kv + 1),
        in_specs=in_specs,
        out_specs=pl.BlockSpec((None, tq, width), lambda b, i, j, f: (b, i + q_off, 0)),
        scratch_shapes=[pltpu.VMEM((hq, V_ROWS, tq), F32), pltpu.VMEM((hq, 8, tq), F32)],
    )
    return pl.pallas_call(
        body,
        grid_spec=grid_spec,
        out_shape=jax.ShapeDtypeStruct((bsz, t, width), F32),
        input_output_aliases=aliases,
        compiler_params=_cparams(("parallel", "parallel", "arbitrary")),
        name=f"flash_attn_h{hq}_d{dq}_{'lat' if latent else 'ctx'}",
    )(*args)


def _attend(flag, q, k, v, *, n_lat, n_ctx, need_ctx):
    o = _attention(flag, q, k, v, None, n_lat=n_lat, n_ctx=n_ctx, latent=True)
    if need_ctx:
        o = _attention(flag, q, k, v, o, n_lat=n_lat, n_ctx=n_ctx, latent=False)
    return o


def _hgrn_kernel(zq_ref, zx_ref, zi_ref, lb_ref, tri_ref, ones_ref, o_ref,
                 st_scr, qt_scr, kt_scr, g_scr, q_scr, k_scr, dec_scr):
    d = pl.program_id(1)
    j = pl.program_id(2)

    @pl.when(j == 0)
    def _():
        st_scr[...] = jnp.zeros(st_scr.shape, F32)

    x = zx_ref[...]
    lb = lb_ref[...]
    log_lb, log_1m, one_m = lb[0:1], lb[1:2], lb[2:3]
    e = jnp.exp(-jnp.abs(x))
    log_sig = jnp.minimum(x, 0.0) - jnp.log1p(e)
    b = log_1m + log_sig
    big = jnp.maximum(log_lb, b)
    logf = big + jnp.log1p(jnp.exp(-jnp.abs(log_lb - b)))
    kk = one_m * (1.0 / (1.0 + jnp.exp(x)))
    q = _silu(zq_ref[...])

    h1 = logf.astype(BF16)
    r1 = logf - h1.astype(F32)
    h2 = r1.astype(BF16)
    h3 = (r1 - h2.astype(F32)).astype(BF16)
    tri = tri_ref[...]
    ones = ones_ref[...]
    g = _dot(tri, h1) + _dot(tri, h2) + _dot(tri, h3)
    gtot = _dot(ones, h1) + _dot(ones, h2) + _dot(ones, h3)
    qt_scr[...] = q * jnp.exp(g)
    kt_scr[...] = kk * jnp.exp(gtot - g)
    dec_scr[...] = jnp.exp(gtot)
    g_scr[...] = g
    q_scr[...] = q
    k_scr[...] = kk

    sgn = 1 - 2 * d
    n_sub = HG_BLOCK // HG_SUB
    s_iota = lax.broadcasted_iota(jnp.int32, (HG_SUB, HG_FDIM), 0)

    def body(n, carry):
        ne = jnp.where(d == 0, n, n_sub - 1 - n)
        r0 = pl.multiple_of(ne * HG_SUB, HG_SUB)
        rows = pl.ds(r0, HG_SUB)
        qt_n = qt_scr[rows, :]
        kt_n = kt_scr[rows, :]
        g_n = g_scr[rows, :]
        q_n = q_scr[rows, :]
        k_n = k_scr[rows, :]
        v_n = zi_ref[rows, 0:GROUP_WIDTH]
        dec_n = dec_scr[pl.ds(r0, 1), :]
        st = st_scr[...]
        stb = st.astype(BF16)
        qtb = qt_n.astype(BF16)
        ktb = kt_n.astype(BF16)
        vb = v_n.astype(BF16)

        rows_out = [[] for _ in range(HG_HEADS)]
        for t in range(HG_SUB):
            allowed = ((s_iota - t) * sgn) <= 0
            diff = jnp.where(allowed, g_n[t:t + 1, :] - g_n, -jnp.inf)
            w = jnp.exp(diff) * k_n * q_n[t:t + 1, :]
            for hh in range(HG_HEADS):
                c = jnp.sum(w[:, HG_EXPAND * hh:HG_EXPAND * (hh + 1)], axis=1, keepdims=True)
                rows_out[hh].append(jnp.sum(c * v_n[:, HG_VDIM * hh:HG_VDIM * (hh + 1)], axis=0, keepdims=True))

        outs = []
        upd = []
        for hh in range(HG_HEADS):
            ks = slice(HG_EXPAND * hh, HG_EXPAND * (hh + 1))
            vs = slice(HG_VDIM * hh, HG_VDIM * (hh + 1))
            inter = lax.dot_general(qtb[:, ks], stb[:, ks], (((1,), (1,)), ((), ())), preferred_element_type=F32)
            outs.append(inter + jnp.concatenate(rows_out[hh], axis=0))
            upd.append(lax.dot_general(vb[:, vs], ktb[:, ks], (((0,), (0,)), ((), ())), preferred_element_type=F32))
        o_ref[rows, :] = jnp.concatenate(outs, axis=1)
        st_scr[...] = st * dec_n + jnp.concatenate(upd, axis=1)
        return carry

    lax.fori_loop(0, n_sub, body, 0)


def _hgrn(zc, lbp, tri, ones, *, n_lat, n_ctx):
    bsz, t, _ = zc.shape
    nl = n_lat // HG_BLOCK
    nc = n_ctx // HG_BLOCK

    def blk(d, j):
        ctx = jnp.where(d == 0, nl + j, nl + nc - 1 - j)
        lat = jnp.where(d == 0, j - nc, nl - 1 - (j - nc))
        return jnp.where(j < nc, ctx, lat)

    return pl.pallas_call(
        _hgrn_kernel,
        grid=(bsz, 2, nl + nc),
        in_specs=[
            pl.BlockSpec((None, HG_BLOCK, HG_FDIM), lambda b, d, j: (b, blk(d, j), 0)),
            pl.BlockSpec((None, HG_BLOCK, HG_FDIM), lambda b, d, j: (b, blk(d, j), 1 + d)),
            pl.BlockSpec((None, HG_BLOCK, HG_FDIM), lambda b, d, j: (b, blk(d, j), 3)),
            pl.BlockSpec((None, 8, HG_FDIM), lambda b, d, j: (d, 0, 0)),
            pl.BlockSpec((None, HG_BLOCK, HG_BLOCK), lambda b, d, j: (d, 0, 0)),
            _resident((HG_BLOCK, HG_BLOCK)),
        ],
        out_specs=pl.BlockSpec((None, None, HG_BLOCK, GROUP_WIDTH), lambda b, d, j: (b, d, blk(d, j), 0)),
        out_shape=jax.ShapeDtypeStruct((bsz, 2, t, GROUP_WIDTH), F32),
        scratch_shapes=[pltpu.VMEM((HG_VDIM, HG_FDIM), F32)]
        + [pltpu.VMEM((HG_BLOCK, HG_FDIM), F32) for _ in range(6)],
        compiler_params=_cparams(("parallel", "parallel", "arbitrary")),
        name="hgrn2_scan",
    )(zc, zc, zc, lbp, tri, ones)


def _out_kernel(x_ref, mod_ref, oa_ref, ob_ref, od_ref, ocf_ref, ocb_ref, g_ref, w_ref, pv_ref, seg64_ref, o_ref):
    x = x_ref[...]
    m = mod_ref[...]
    seg64 = seg64_ref[...]
    lam = pv_ref[:, O_LAM:O_LAM + LANES]

    oa = oa_ref[...]
    lane = lax.broadcasted_iota(jnp.int32, (x.shape[0], LANES), 1)
    ya = []
    for c in range(DA_HEADS):
        pair = oa[:, c * LANES:(c + 1) * LANES]
        diff = pair - lam * pltpu.roll(pair, HEAD_DIM, 1)
        ya.append(jnp.where(lane < HEAD_DIM, diff, 0.0))
    ya = jnp.concatenate(ya, axis=1)
    ya = jnp.concatenate([_seg_norm(ya[:, 0:256], seg64, 64), _seg_norm(ya[:, 256:512], seg64, 64)], axis=1)
    ya = ya * pv_ref[:, O_SUBLN:O_SUBLN + 512]

    oc = ocf_ref[...] + ocb_ref[...]
    yc = _seg_norm(oc, seg64, 64) * pv_ref[:, O_HGN:O_HGN + 256] * _silu(g_ref[...])

    acc = _dot(ya.astype(BF16), w_ref[0:512, :])
    acc = acc + _dot(ob_ref[...].astype(BF16), w_ref[512:768, :])
    acc = acc + _dot(yc.astype(BF16), w_ref[768:1024, :])
    acc = acc + _dot(od_ref[...].astype(BF16), w_ref[1024:1280, :])
    o_ref[...] = x + m[5:6] * acc


def _out_proj(xs, modsel, oa, ob, od, oc, zc, w, pvec, seg64, *, n_lat_tiles, n_tiles):
    bsz, t, _ = xs.shape
    row = lambda width: pl.BlockSpec((None, ROW_TILE, width), lambda b, i: (b, i, 0))
    return pl.pallas_call(
        _out_kernel,
        grid=(bsz, n_tiles),
        in_specs=[
            row(D_MODEL),
            pl.BlockSpec((None, None, N_MOD, D_MODEL),
                         lambda b, i: (b, jnp.where(i >= n_lat_tiles, 1, 0), 0, 0)),
            row(512), row(256), row(256),
            pl.BlockSpec((None, None, ROW_TILE, GROUP_WIDTH), lambda b, i: (b, 0, i, 0)),
            pl.BlockSpec((None, None, ROW_TILE, GROUP_WIDTH), lambda b, i: (b, 1, i, 0)),
            pl.BlockSpec((None, ROW_TILE, GROUP_WIDTH), lambda b, i: (b, i, C_COLS // GROUP_WIDTH - 1)),
            _resident((1280, D_MODEL)),
            _resident((1, O_LEN)),
            _resident((256, 256)),
        ],
        out_specs=row(D_MODEL),
        out_shape=jax.ShapeDtypeStruct((bsz, t, D_MODEL), F32),
        compiler_params=_cparams(("parallel", "parallel")),
        name="mix_out_proj",
    )(xs, modsel, oa, ob, od, oc, oc, zc, w, pvec, seg64)


def _block_diag_ones(n, size):
    idx = np.arange(n) // size
    return jnp.asarray((idx[:, None] == idx[None, :]).astype(np.float32), dtype=BF16)


def _rope_table(n_lat, n_ctx):
    rows = n_lat // GRID_W
    r = jnp.repeat(jnp.arange(rows), GRID_W).astype(F32)
    col = jnp.tile(jnp.arange(GRID_W), rows).astype(F32)

    def pattern(rot_dim):
        n_freq = rot_dim // 4
        inv = ROPE_THETA ** (-jnp.arange(n_freq, dtype=F32) / n_freq)
        ang = jnp.concatenate([r[:, None] * inv, col[:, None] * inv], axis=-1)
        cos, sin = jnp.cos(ang), jnp.sin(ang)
        cos_l = jnp.tile(jnp.concatenate([cos, cos], axis=-1), (1, LANES // rot_dim))
        sin_l = jnp.tile(jnp.concatenate([-sin, sin], axis=-1), (1, LANES // rot_dim))
        cos_l = jnp.concatenate([cos_l, jnp.ones((n_ctx, LANES), F32)], axis=0)
        sin_l = jnp.concatenate([sin_l, jnp.zeros((n_ctx, LANES), F32)], axis=0)
        return cos_l, sin_l

    c32, s32 = pattern(32)
    c64, s64 = pattern(64)
    return jnp.concatenate([c32, s32, c64, s64], axis=1)


def _tile_lanes(v, reps):
    return jnp.tile(v.astype(F32), reps)


def kernel(x, c, ctx, c_ctx, ada_w, ada_b, norm_w, ffn_w_in, ffn_w_out, mix_w_in, mix_w_out, da_qk_norm, da_lambda,
           da_subln, mla_q_norm, mla_kv_norm, mla_w_uq, mla_w_ukv, mla_nope_norm, mla_rope_norm, hg_lb_logits,
           hg_out_norm, gq_qk_norm):
    bsz, n_lat, _ = x.shape
    n_ctx = ctx.shape[1]
    depth = ada_w.shape[0]
    assert bsz + 1 <= 8 and n_lat % ATT_TK == 0 and n_lat % ATT_TQ == 0 and n_lat % n_ctx == 0
    assert n_ctx % ROW_TILE == 0 and n_ctx % HG_BLOCK == 0
    t = n_lat + n_ctx
    n_lat_tiles = n_lat // ROW_TILE
    n_tiles = t // ROW_TILE

    xs = jnp.concatenate([x, ctx], axis=1)

    cc = jnp.zeros((8, D_MODEL), F32).at[0:bsz].set(c).at[bsz].set(c_ctx)
    mods = _modulation(cc, ada_w, ada_b).reshape(depth, 8, N_MOD, D_MODEL)

    p = jax.nn.softmax(hg_lb_logits.astype(F32), axis=1)
    lbs = jnp.maximum(jnp.cumsum(p, axis=1) - p[:, :1], 0.0)

    rope = _rope_table(n_lat, n_ctx)
    seg32 = _block_diag_ones(256, 32)
    seg64 = _block_diag_ones(256, 64)
    ones_sub = _block_diag_ones(HG_BLOCK, HG_SUB)
    ii = np.arange(HG_BLOCK)
    same = (ii[:, None] // HG_SUB) == (ii[None, :] // HG_SUB)
    tri = jnp.asarray(np.stack([same & (ii[None, :] <= ii[:, None]), same & (ii[None, :] >= ii[:, None])])
                      .astype(np.float32), dtype=BF16)

    def group_cols(w, first):
        r = w.shape[0]
        w3 = w.reshape(r, MLA_HEADS, -1)
        return jnp.concatenate([w3[:, :, :first].reshape(r, -1), w3[:, :, first:].reshape(r, -1)], axis=1)

    def fast_flag(qn2, kn2, scale):
        bound = jnp.sqrt(qn2 * kn2) * (scale * LOG2E * 1.02)
        return (bound <= FAST_SCORE_BOUND).astype(jnp.int32).reshape(1)

    def gmax2(g):
        return jnp.max(jnp.abs(g.astype(F32))) ** 2

    for l in range(depth):
        modsel = jnp.stack([mods[l, 0:bsz], jnp.broadcast_to(mods[l, bsz], (bsz, N_MOD, D_MODEL))], axis=1)
        nw = norm_w[l].astype(F32)
        last = l == depth - 1

        xs = _ffn(xs, modsel, nw, ffn_w_in[l, 0].astype(BF16), ffn_w_out[l, 0].astype(BF16),
                  sub=0, n_lat_tiles=n_lat_tiles, n_tiles=n_tiles)

        wl = mix_w_in[l]
        w_in = jnp.concatenate(
            [wl[:, 0:A_COLS], wl[:, A_COLS:A_COLS + B_COLS], jnp.zeros((D_MODEL, B_PAD - B_COLS), F32),
             wl[:, A_COLS + B_COLS:]], axis=1).astype(BF16)
        pvec = jnp.concatenate([
            _tile_lanes(da_qk_norm[l, 0], 8), _tile_lanes(da_qk_norm[l, 1], 8),
            mla_q_norm[l].astype(F32), mla_kv_norm[l].astype(F32),
            _tile_lanes(mla_nope_norm[l, 0], 4), _tile_lanes(mla_nope_norm[l, 1], 4),
            _tile_lanes(mla_rope_norm[l, 0], 4), _tile_lanes(mla_rope_norm[l, 1], 4),
            _tile_lanes(gq_qk_norm[l, 0], 4), _tile_lanes(gq_qk_norm[l, 1], 2)]).reshape(1, P_LEN)
        qa, ka, va, qb, kb, vb, qd, kd, vd, zc = _prep(
            xs, modsel, nw, w_in, rope, seg32, seg64, pvec,
            group_cols(mla_w_uq[l], MLA_NOPE_DIM).astype(BF16), group_cols(mla_w_ukv[l], MLA_NOPE_DIM).astype(BF16),
            n_lat_tiles=n_lat_tiles)

        fa = fast_flag(DA_QK_DIM * gmax2(da_qk_norm[l, 0]), DA_QK_DIM * gmax2(da_qk_norm[l, 1]), DA_QK_DIM ** -0.5)
        fb = fast_flag(MLA_NOPE_DIM * gmax2(mla_nope_norm[l, 0]) + MLA_ROPE_DIM * gmax2(mla_rope_norm[l, 0]),
                       MLA_NOPE_DIM * gmax2(mla_nope_norm[l, 1]) + MLA_ROPE_DIM * gmax2(mla_rope_norm[l, 1]),
                       (MLA_NOPE_DIM + MLA_ROPE_DIM) ** -0.5)
        fd = fast_flag(HEAD_DIM * gmax2(gq_qk_norm[l, 0]), HEAD_DIM * gmax2(gq_qk_norm[l, 1]), HEAD_DIM ** -0.5)
        oa = _attend(fa, qa, ka, va, n_lat=n_lat, n_ctx=n_ctx, need_ctx=not last)
        ob = _attend(fb, qb, kb, vb, n_lat=n_lat, n_ctx=n_ctx, need_ctx=not last)
        od = _attend(fd, qd, kd, vd, n_lat=n_lat, n_ctx=n_ctx, need_ctx=not last)

        lb = lbs[:, l]
        lbp = jnp.zeros((2, 8, HG_FDIM), F32)
        lbp = lbp.at[:, 0].set(jnp.log(lb)).at[:, 1].set(jnp.log1p(-lb)).at[:, 2].set(1.0 - lb)
        oc = _hgrn(zc, lbp, tri, ones_sub, n_lat=n_lat, n_ctx=n_ctx)

        lam_init = 0.8 - 0.6 * math.exp(-0.3 * l)
        lp = da_lambda[l].astype(F32)
        lam = jnp.exp(jnp.sum(lp[0] * lp[1])) - jnp.exp(jnp.sum(lp[2] * lp[3])) + lam_init
        subln = jnp.concatenate([da_subln[l].astype(F32) * (1.0 - lam_init), jnp.zeros((HEAD_DIM,), F32)])
        opv = jnp.concatenate([jnp.tile(subln, DA_HEADS), _tile_lanes(hg_out_norm[l], HG_HEADS),
                               jnp.full((LANES,), lam, F32)]).reshape(1, O_LEN)
        wo = mix_w_out[l]
        wa = jnp.concatenate([wo[0:GROUP_WIDTH].reshape(DA_HEADS, HEAD_DIM, D_MODEL),
                              jnp.zeros((DA_HEADS, HEAD_DIM, D_MODEL), F32)], axis=1).reshape(2 * GROUP_WIDTH, D_MODEL)
        w_out = jnp.concatenate([wa, wo[GROUP_WIDTH:]], axis=0).astype(BF16)
        xs = _out_proj(xs, modsel, oa, ob, od, oc, zc, w_out, opv, seg64, n_lat_tiles=n_lat_tiles,
                       n_tiles=n_lat_tiles if last else n_tiles)

        xs = _ffn(xs, modsel, nw, ffn_w_in[l, 1].astype(BF16), ffn_w_out[l, 1].astype(BF16),
                  sub=2, n_lat_tiles=n_lat_tiles, n_tiles=n_lat_tiles if last else n_tiles)
    return xs
```

```python
import functools
import math

import numpy as np
import jax
import jax.numpy as jnp
from jax import lax
from jax.experimental import pallas as pl
from jax.experimental.pallas import tpu as pltpu

F32 = jnp.float32
BF16 = jnp.bfloat16

D_MODEL = 1024
HEAD_DIM = 64
GROUP_WIDTH = D_MODEL // 4
EPS = 1e-6
ROPE_THETA = 10000.0
GRID_W = 64
HALF = 0.5
N_MOD = 9
D_FF = 2816

DA_HEADS = 4
DA_QK_DIM = 32
MLA_HEADS = 4
MLA_Q_RANK = 256
MLA_KV_RANK = 128
MLA_NOPE_DIM = 64
MLA_ROPE_DIM = 32
HG_HEADS = 4
HG_EXPAND = 128
HG_FDIM = 512
HG_VDIM = 64
GQ_HEADS = 4
GQ_KV_HEADS = 2

A_COLS = 768
B_COLS = 416
C_COLS = 2048
D_COLS = 512
B_PAD = 512
W_COLS = A_COLS + B_PAD + C_COLS + D_COLS

LANES = 128
ROW_TILE = 512
ATT_TQ = 1024
ATT_TK = 1024
V_ROWS = HEAD_DIM
LOG2E = 1.4426950408889634
FAST_SCORE_BOUND = 64.0
HG_BLOCK = 256
HG_SUB = 16
FF_CHUNK = 512
VMEM_LIMIT = 56 * 1024 * 1024

P_GAQ, P_GAK, P_QN, P_KVN, P_NOPE0, P_NOPE1, P_ROPE0, P_ROPE1, P_GDQ, P_GDK, P_LEN = (
    0, 256, 512, 768, 896, 1152, 1408, 1536, 1664, 1920, 2048)
O_SUBLN, O_HGN, O_LAM, O_LEN = 0, 512, 768, 896


def _cparams(sem):
    return pltpu.CompilerParams(dimension_semantics=sem, vmem_limit_bytes=VMEM_LIMIT)


def _resident(shape):
    nd = len(shape)
    return pl.BlockSpec(shape, lambda *_: (0,) * nd, pipeline_mode=pl.Buffered(1))


def _dot(a, b):
    return jnp.dot(a, b, preferred_element_type=F32)


def _rms_rows(x):
    return x * lax.rsqrt(jnp.mean(x * x, axis=-1, keepdims=True) + EPS)


def _seg_norm(x, seg, size):
    x2 = x * x
    hi = x2.astype(BF16)
    lo = (x2 - hi.astype(F32)).astype(BF16)
    ss = _dot(hi, seg) + _dot(lo, seg)
    return x * lax.rsqrt(ss * (1.0 / size) + EPS)


def _swap_halves(x, half):
    lane = lax.broadcasted_iota(jnp.int32, x.shape, 1)
    up = pltpu.roll(x, LANES - half, 1)
    down = pltpu.roll(x, half, 1)
    return jnp.where((lane % (2 * half)) < half, up, down)


def _rope(x, cos, sin, half):
    parts = []
    for c in range(x.shape[1] // LANES):
        xs = x[:, c * LANES:(c + 1) * LANES]
        parts.append(xs * cos + _swap_halves(xs, half) * sin)
    return parts[0] if len(parts) == 1 else jnp.concatenate(parts, axis=1)


def _silu(x):
    return x / (1.0 + jnp.exp(-x))


def _mod_kernel(c_ref, w_ref, b_ref, o_ref):
    a = _silu(c_ref[...])
    a_hi = a.astype(BF16)
    a_lo = (a - a_hi.astype(F32)).astype(BF16)
    w = w_ref[...]
    w_hi = w.astype(BF16)
    w_lo = (w - w_hi.astype(F32)).astype(BF16)
    o_ref[...] = _dot(a_hi, w_hi) + _dot(a_hi, w_lo) + _dot(a_lo, w_hi) + b_ref[...]


def _modulation(cc, ada_w, ada_b):
    depth = ada_w.shape[0]
    nb = N_MOD * D_MODEL // D_MODEL
    return pl.pallas_call(
        _mod_kernel,
        grid=(depth, nb),
        in_specs=[
            pl.BlockSpec((8, D_MODEL), lambda l, j: (0, 0)),
            pl.BlockSpec((None, D_MODEL, D_MODEL), lambda l, j: (l, 0, j)),
            pl.BlockSpec((None, 1, D_MODEL), lambda l, j: (l, 0, j)),
        ],
        out_specs=pl.BlockSpec((None, 8, D_MODEL), lambda l, j: (l, 0, j)),
        out_shape=jax.ShapeDtypeStruct((depth, 8, N_MOD * D_MODEL), F32),
        compiler_params=_cparams(("parallel", "parallel")),
        name="adaln_mod",
    )(cc, ada_w, ada_b.reshape(depth, 1, N_MOD * D_MODEL))


def _ffn_kernel(x_ref, mod_ref, nw_ref, win_ref, wout_ref, o_ref, *, sub):
    x = x_ref[...]
    m = mod_ref[...]
    shift, scale, gate = m[3 * sub:3 * sub + 1], m[3 * sub + 1:3 * sub + 2], m[3 * sub + 2:3 * sub + 3]
    h = (_rms_rows(x) * nw_ref[sub:sub + 1]) * (1.0 + scale) + shift
    hb = h.astype(BF16)
    acc = jnp.zeros(x.shape, F32)
    for lo in range(0, D_FF, FF_CHUNK):
        hi = min(lo + FF_CHUNK, D_FF)
        g = _dot(hb, win_ref[:, lo:hi])
        u = _dot(hb, win_ref[:, D_FF + lo:D_FF + hi])
        a = (_silu(g) * u).astype(BF16)
        acc = acc + _dot(a, wout_ref[lo:hi, :])
    o_ref[...] = x + HALF * gate * acc


def _ffn(xs, modsel, nw, win, wout, *, sub, n_lat_tiles, n_rows):
    bsz = xs.shape[0]
    return pl.pallas_call(
        functools.partial(_ffn_kernel, sub=sub),
        grid=(bsz, pl.cdiv(n_rows, ROW_TILE)),
        in_specs=[
            pl.BlockSpec((None, ROW_TILE, D_MODEL), lambda b, i: (b, i, 0)),
            pl.BlockSpec((None, None, N_MOD, D_MODEL),
                         lambda b, i: (b, jnp.where(i >= n_lat_tiles, 1, 0), 0, 0)),
            _resident((3, D_MODEL)),
            _resident((D_MODEL, 2 * D_FF)),
            _resident((D_FF, D_MODEL)),
        ],
        out_specs=pl.BlockSpec((None, ROW_TILE, D_MODEL), lambda b, i: (b, i, 0)),
        out_shape=jax.ShapeDtypeStruct((bsz, n_rows, D_MODEL), F32),
        compiler_params=_cparams(("parallel", "parallel")),
        name=f"ffn_half_{sub}",
    )(xs, modsel, nw, win, wout)


def _prep_kernel(x_ref, mod_ref, nw_ref, w_ref, rope_ref, seg32_ref, seg64_ref, pv_ref, wuq_ref, wukv_ref,
                 qa_ref, ka_ref, va_ref, qb_ref, kb_ref, vb_ref, qd_ref, kd_ref, vd_ref, zc_ref):
    x = x_ref[...]
    m = mod_ref[...]
    h = (_rms_rows(x) * nw_ref[1:2]) * (1.0 + m[4:5]) + m[3:4]
    hb = h.astype(BF16)
    seg32 = seg32_ref[...]
    seg64 = seg64_ref[...]
    cos32, sin32 = rope_ref[:, 0:128], rope_ref[:, 128:256]
    cos64, sin64 = rope_ref[:, 256:384], rope_ref[:, 384:512]

    def pv(lo, hi):
        return pv_ref[:, lo:hi]

    za = _dot(hb, w_ref[:, 0:A_COLS])
    qa = _rope(_seg_norm(za[:, 0:256], seg32, 32) * pv(P_GAQ, P_GAQ + 256), cos32, sin32, 16)
    qa = qa * (DA_QK_DIM ** -0.5 * LOG2E)
    ka = _rope(_seg_norm(za[:, 256:512], seg32, 32) * pv(P_GAK, P_GAK + 256), cos32, sin32, 16)
    qa_t = jnp.transpose(qa)
    for hh in range(2 * DA_HEADS):
        qa_ref[hh] = qa_t[32 * hh:32 * hh + 32, :].astype(BF16)
        ka_ref[hh] = ka[:, 32 * hh:32 * hh + 32].astype(BF16)

    def put_v(ref, hh, val_t):
        ref[hh] = val_t.astype(BF16)

    va_t = jnp.transpose(za[:, 512:768])
    for hh in range(DA_HEADS):
        put_v(va_ref, hh, va_t[64 * hh:64 * hh + 64, :])

    zb = _dot(hb, w_ref[:, A_COLS:A_COLS + B_PAD])
    cq = _rms_rows(zb[:, 0:256]) * pv(P_QN, P_QN + 256)
    q2 = _dot(cq.astype(BF16), wuq_ref[...])
    ckv = _rms_rows(zb[:, 256:384]) * pv(P_KVN, P_KVN + 128)
    kv2 = _dot(ckv.astype(BF16), wukv_ref[...])
    sb = (MLA_NOPE_DIM + MLA_ROPE_DIM) ** -0.5 * LOG2E
    qn = _seg_norm(q2[:, 0:256], seg64, 64) * pv(P_NOPE0, P_NOPE0 + 256) * sb
    kn = _seg_norm(kv2[:, 0:256], seg64, 64) * pv(P_NOPE1, P_NOPE1 + 256)
    seg32s = seg32[0:128, 0:128]
    qr = _rope(_seg_norm(q2[:, 256:384], seg32s, 32) * pv(P_ROPE0, P_ROPE0 + 128), cos32, sin32, 16) * sb
    kr = _rope(_seg_norm(zb[:, 384:512], seg32s, 32) * pv(P_ROPE1, P_ROPE1 + 128), cos32, sin32, 16)
    zpad = jnp.zeros((x.shape[0], 32), BF16)
    zpad_t = jnp.zeros((32, x.shape[0]), BF16)
    qn_t = jnp.transpose(qn)
    qr_t = jnp.transpose(qr)
    vb_t = jnp.transpose(kv2[:, 256:512])
    for hh in range(MLA_HEADS):
        qb_ref[hh, 0:64, :] = qn_t[64 * hh:64 * hh + 64, :].astype(BF16)
        qb_ref[hh, 64:96, :] = qr_t[32 * hh:32 * hh + 32, :].astype(BF16)
        qb_ref[hh, 96:128, :] = zpad_t
        kb_ref[hh, :, 0:64] = kn[:, 64 * hh:64 * hh + 64].astype(BF16)
        kb_ref[hh, :, 64:96] = kr[:, 0:32].astype(BF16)
        kb_ref[hh, :, 96:128] = zpad
        put_v(vb_ref, hh, vb_t[64 * hh:64 * hh + 64, :])

    zc_ref[...] = _dot(hb, w_ref[:, A_COLS + B_PAD:A_COLS + B_PAD + C_COLS])

    zd = _dot(hb, w_ref[:, A_COLS + B_PAD + C_COLS:W_COLS])
    qd = _rope(_seg_norm(zd[:, 0:256], seg64, 64) * pv(P_GDQ, P_GDQ + 256), cos64, sin64, 32)
    qd = qd * (HEAD_DIM ** -0.5 * LOG2E)
    kd = _rope(_seg_norm(zd[:, 256:384], seg64[0:128, 0:128], 64) * pv(P_GDK, P_GDK + 128), cos64, sin64, 32)
    qd_t = jnp.transpose(qd)
    vd_t = jnp.transpose(zd[:, 384:512])
    for hh in range(GQ_HEADS):
        qd_ref[hh] = qd_t[64 * hh:64 * hh + 64, :].astype(BF16)
    for hh in range(GQ_KV_HEADS):
        kd_ref[hh] = kd[:, 64 * hh:64 * hh + 64].astype(BF16)
        put_v(vd_ref, hh, vd_t[64 * hh:64 * hh + 64, :])


def _prep(xs, modsel, nw, w, rope, seg32, seg64, pvec, wuq, wukv, *, n_lat_tiles):
    bsz, t, _ = xs.shape
    n_tiles = pl.cdiv(t, ROW_TILE)

    def heads(h, d):
        return (pl.BlockSpec((None, h, ROW_TILE, d), lambda b, i: (b, 0, i, 0)),
                jax.ShapeDtypeStruct((bsz, h, t, d), BF16))

    def heads_t(h, d):
        return (pl.BlockSpec((None, h, d, ROW_TILE), lambda b, i: (b, 0, 0, i)),
                jax.ShapeDtypeStruct((bsz, h, d, t), BF16))

    outs = [heads_t(8, 32), heads(8, 32), heads_t(4, V_ROWS),
            heads_t(4, 128), heads(4, 128), heads_t(4, V_ROWS),
            heads_t(4, 64), heads(2, 64), heads_t(2, V_ROWS),
            (pl.BlockSpec((None, ROW_TILE, C_COLS), lambda b, i: (b, i, 0)),
             jax.ShapeDtypeStruct((bsz, t, C_COLS), F32))]
    return pl.pallas_call(
        _prep_kernel,
        grid=(bsz, n_tiles),
        in_specs=[
            pl.BlockSpec((None, ROW_TILE, D_MODEL), lambda b, i: (b, i, 0)),
            pl.BlockSpec((None, None, N_MOD, D_MODEL),
                         lambda b, i: (b, jnp.where(i >= n_lat_tiles, 1, 0), 0, 0)),
            _resident((3, D_MODEL)),
            _resident((D_MODEL, W_COLS)),
            pl.BlockSpec((ROW_TILE, 512), lambda b, i: (i, 0)),
            _resident((256, 256)),
            _resident((256, 256)),
            _resident((1, P_LEN)),
            _resident((MLA_Q_RANK, 384)),
            _resident((MLA_KV_RANK, 512)),
        ],
        out_specs=[o[0] for o in outs],
        out_shape=[o[1] for o in outs],
        compiler_params=_cparams(("parallel", "parallel")),
        name="mix_in_prep",
    )(xs, modsel, nw, w, rope, seg32, seg64, pvec, wuq, wukv)


def _attn_kernel(flag_ref, q_ref, k_ref, v_ref, kc_ref, vc_ref, o_ref, acc_scr, l_scr, m_scr, *, hq, hk, hv, n_kv):
    j = pl.program_id(2)
    fast = flag_ref[0] == 1

    @pl.when(j == 0)
    def _():
        acc_scr[...] = jnp.zeros(acc_scr.shape, F32)
        l_scr[...] = jnp.zeros(l_scr.shape, F32)
        m_scr[...] = jnp.full(m_scr.shape, -jnp.inf, F32)

    def scores(h, kr):
        return _dot(kr[h // (hq // hk)], q_ref[h])

    def step_fast(kr, vr):
        for h in range(hq):
            p = jnp.exp2(scores(h, kr))
            l_scr[h] += jnp.sum(p, axis=0, keepdims=True)
            acc_scr[h] += _dot(vr[h // (hq // hv)], p.astype(BF16))

    def step_safe(kr, vr):
        for h in range(hq):
            s = scores(h, kr)
            m_prev = m_scr[h]
            m_new = jnp.maximum(m_prev, jnp.max(s, axis=0, keepdims=True))
            alpha = jnp.exp2(m_prev - m_new)
            p = jnp.exp2(s - m_new[0:1, :])
            l_scr[h] = alpha * l_scr[h] + jnp.sum(p, axis=0, keepdims=True)
            acc_scr[h] = alpha[0:1, :] * acc_scr[h] + _dot(vr[h // (hq // hv)], p.astype(BF16))
            m_scr[h] = m_new

    def both(kr, vr):
        @pl.when(fast)
        def _():
            step_fast(kr, vr)

        @pl.when(jnp.logical_not(fast))
        def _():
            step_safe(kr, vr)

    @pl.when(j < n_kv)
    def _():
        both(k_ref, v_ref)

    @pl.when(j == n_kv)
    def _():
        both(kc_ref, vc_ref)
        outs = []
        for h in range(hq):
            outs.append(acc_scr[h] / l_scr[h][0:1, :])
        o_ref[...] = jnp.transpose(jnp.concatenate(outs, axis=0))


def _attention(flag, q, k, v, prev, *, n_lat, n_ctx, latent):
    bsz, hq, dq, t = q.shape
    hk, hv = k.shape[1], v.shape[1]
    width = hq * HEAD_DIM
    ctx_blk = n_lat // n_ctx
    if latent:
        tq, nq, n_kv, q_off = ATT_TQ, n_lat // ATT_TQ, n_lat // ATT_TK, 0
    else:
        tq, nq, n_kv, q_off = n_ctx, 1, 0, ctx_blk

    def kv_blk(j):
        return jnp.minimum(j, max(n_kv - 1, 0))

    in_specs = [
        pl.BlockSpec((None, hq, dq, tq), lambda b, i, j, f: (b, 0, 0, i + q_off)),
        pl.BlockSpec((None, hk, ATT_TK, dq), lambda b, i, j, f: (b, 0, kv_blk(j), 0)),
        pl.BlockSpec((None, hv, V_ROWS, ATT_TK), lambda b, i, j, f: (b, 0, 0, kv_blk(j))),
        pl.BlockSpec((None, hk, n_ctx, dq), lambda b, i, j, f: (b, 0, ctx_blk, 0)),
        pl.BlockSpec((None, hv, V_ROWS, n_ctx), lambda b, i, j, f: (b, 0, 0, ctx_blk)),
    ]
    args = [flag, q, k, v, k, v]
    aliases = {}
    if not latent:
        in_specs.append(pl.BlockSpec(memory_space=pl.ANY))
        args.append(prev)
        aliases = {6: 0}

    def body(flag_ref, q_ref, k_ref, v_ref, kc_ref, vc_ref, *rest):
        o_ref, acc_scr, l_scr, m_scr = rest[-4:]
        _attn_kernel(flag_ref, q_ref, k_ref, v_ref, kc_ref, vc_ref, o_ref, acc_scr, l_scr, m_scr,
                     hq=hq, hk=hk, hv=hv, n_kv=n_kv)

    grid_spec = pltpu.PrefetchScalarGridSpec(
        num_scalar_prefetch=1,
        grid=(bsz, nq, n_kv + 1),
        in_specs=in_specs,
        out_specs=pl.BlockSpec((None, tq, width), lambda b, i, j, f: (b, i + q_off, 0)),
        scratch_shapes=[pltpu.VMEM((hq, HEAD_DIM, tq), F32), pltpu.VMEM((hq, 8, tq), F32),
                        pltpu.VMEM((hq, 8, tq), F32)],
    )
    return pl.pallas_call(
        body,
        grid_spec=grid_spec,
        out_shape=jax.ShapeDtypeStruct((bsz, t, width), F32),
        input_output_aliases=aliases,
        compiler_params=_cparams(("parallel", "parallel", "arbitrary")),
        name=f"flash_attn_h{hq}_d{dq}_{'lat' if latent else 'ctx'}",
    )(*args)


def _attend(flag, q, k, v, *, n_lat, n_ctx, need_ctx):
    o = _attention(flag, q, k, v, None, n_lat=n_lat, n_ctx=n_ctx, latent=True)
    if need_ctx:
        o = _attention(flag, q, k, v, o, n_lat=n_lat, n_ctx=n_ctx, latent=False)
    return o


def _hgrn_kernel(zq_ref, zx_ref, zi_ref, lb_ref, tri_ref, ones_ref, o_ref,
                 st_scr, qt_scr, kt_scr, g_scr, q_scr, k_scr, dec_scr):
    d = pl.program_id(1)
    j = pl.program_id(2)

    @pl.when(j == 0)
    def _():
        st_scr[...] = jnp.zeros(st_scr.shape, F32)

    x = zx_ref[...]
    lb = lb_ref[...]
    log_lb, log_1m, one_m = lb[0:1], lb[1:2], lb[2:3]
    e = jnp.exp(-jnp.abs(x))
    log_sig = jnp.minimum(x, 0.0) - jnp.log1p(e)
    b = log_1m + log_sig
    big = jnp.maximum(log_lb, b)
    logf = big + jnp.log1p(jnp.exp(-jnp.abs(log_lb - b)))
    kk = one_m * (1.0 / (1.0 + jnp.exp(x)))
    q = _silu(zq_ref[...])

    h1 = logf.astype(BF16)
    r1 = logf - h1.astype(F32)
    h2 = r1.astype(BF16)
    h3 = (r1 - h2.astype(F32)).astype(BF16)
    tri = tri_ref[...]
    ones = ones_ref[...]
    g = _dot(tri, h1) + _dot(tri, h2) + _dot(tri, h3)
    gtot = _dot(ones, h1) + _dot(ones, h2) + _dot(ones, h3)
    qt_scr[...] = q * jnp.exp(g)
    kt_scr[...] = kk * jnp.exp(gtot - g)
    dec_scr[...] = jnp.exp(gtot)
    g_scr[...] = g
    q_scr[...] = q
    k_scr[...] = kk

    sgn = 1 - 2 * d
    n_sub = HG_BLOCK // HG_SUB
    s_iota = lax.broadcasted_iota(jnp.int32, (HG_SUB, HG_FDIM), 0)

    def body(n, carry):
        ne = jnp.where(d == 0, n, n_sub - 1 - n)
        r0 = pl.multiple_of(ne * HG_SUB, HG_SUB)
        rows = pl.ds(r0, HG_SUB)
        qt_n = qt_scr[rows, :]
        kt_n = kt_scr[rows, :]
        g_n = g_scr[rows, :]
        q_n = q_scr[rows, :]
        k_n = k_scr[rows, :]
        v_n = zi_ref[rows, 0:GROUP_WIDTH]
        dec_n = dec_scr[pl.ds(r0, 1), :]
        st = st_scr[...]
        stb = st.astype(BF16)
        qtb = qt_n.astype(BF16)
        ktb = kt_n.astype(BF16)
        vb = v_n.astype(BF16)

        rows_out = [[] for _ in range(HG_HEADS)]
        for t in range(HG_SUB):
            allowed = ((s_iota - t) * sgn) <= 0
            diff = jnp.where(allowed, g_n[t:t + 1, :] - g_n, -jnp.inf)
            w = jnp.exp(diff) * k_n * q_n[t:t + 1, :]
            for hh in range(HG_HEADS):
                c = jnp.sum(w[:, HG_EXPAND * hh:HG_EXPAND * (hh + 1)], axis=1, keepdims=True)
                rows_out[hh].append(jnp.sum(c * v_n[:, HG_VDIM * hh:HG_VDIM * (hh + 1)], axis=0, keepdims=True))

        outs = []
        upd = []
        for hh in range(HG_HEADS):
            ks = slice(HG_EXPAND * hh, HG_EXPAND * (hh + 1))
            vs = slice(HG_VDIM * hh, HG_VDIM * (hh + 1))
            inter = lax.dot_general(qtb[:, ks], stb[:, ks], (((1,), (1,)), ((), ())), preferred_element_type=F32)
            outs.append(inter + jnp.concatenate(rows_out[hh], axis=0))
            upd.append(lax.dot_general(vb[:, vs], ktb[:, ks], (((0,), (0,)), ((), ())), preferred_element_type=F32))
        o_ref[rows, :] = jnp.concatenate(outs, axis=1)
        st_scr[...] = st * dec_n + jnp.concatenate(upd, axis=1)
        return carry

    lax.fori_loop(0, n_sub, body, 0)


def _hgrn(zc, lbp, tri, ones, *, n_lat, n_ctx):
    bsz, t, _ = zc.shape
    nl = n_lat // HG_BLOCK
    nc = n_ctx // HG_BLOCK

    def blk(d, j):
        ctx = jnp.where(d == 0, nl + j, nl + nc - 1 - j)
        lat = jnp.where(d == 0, j - nc, nl - 1 - (j - nc))
        return jnp.where(j < nc, ctx, lat)

    return pl.pallas_call(
        _hgrn_kernel,
        grid=(bsz, 2, nl + nc),
        in_specs=[
            pl.BlockSpec((None, HG_BLOCK, HG_FDIM), lambda b, d, j: (b, blk(d, j), 0)),
            pl.BlockSpec((None, HG_BLOCK, HG_FDIM), lambda b, d, j: (b, blk(d, j), 1 + d)),
            pl.BlockSpec((None, HG_BLOCK, HG_FDIM), lambda b, d, j: (b, blk(d, j), 3)),
            pl.BlockSpec((None, 8, HG_FDIM), lambda b, d, j: (d, 0, 0)),
            pl.BlockSpec((None, HG_BLOCK, HG_BLOCK), lambda b, d, j: (d, 0, 0)),
            _resident((HG_BLOCK, HG_BLOCK)),
        ],
        out_specs=pl.BlockSpec((None, None, HG_BLOCK, GROUP_WIDTH), lambda b, d, j: (b, d, blk(d, j), 0)),
        out_shape=jax.ShapeDtypeStruct((bsz, 2, t, GROUP_WIDTH), F32),
        scratch_shapes=[pltpu.VMEM((HG_VDIM, HG_FDIM), F32)]
        + [pltpu.VMEM((HG_BLOCK, HG_FDIM), F32) for _ in range(6)],
        compiler_params=_cparams(("parallel", "parallel", "arbitrary")),
        name="hgrn2_scan",
    )(zc, zc, zc, lbp, tri, ones)


def _out_kernel(x_ref, mod_ref, oa_ref, ob_ref, od_ref, ocf_ref, ocb_ref, g_ref, w_ref, pv_ref, seg64_ref, o_ref):
    x = x_ref[...]
    m = mod_ref[...]
    seg64 = seg64_ref[...]
    lam = pv_ref[:, O_LAM:O_LAM + LANES]

    oa = oa_ref[...]
    lane = lax.broadcasted_iota(jnp.int32, (x.shape[0], LANES), 1)
    ya = []
    for c in range(DA_HEADS):
        pair = oa[:, c * LANES:(c + 1) * LANES]
        diff = pair - lam * pltpu.roll(pair, HEAD_DIM, 1)
        ya.append(jnp.where(lane < HEAD_DIM, diff, 0.0))
    ya = jnp.concatenate(ya, axis=1)
    ya = jnp.concatenate([_seg_norm(ya[:, 0:256], seg64, 64), _seg_norm(ya[:, 256:512], seg64, 64)], axis=1)
    ya = ya * pv_ref[:, O_SUBLN:O_SUBLN + 512]

    oc = ocf_ref[...] + ocb_ref[...]
    yc = _seg_norm(oc, seg64, 64) * pv_ref[:, O_HGN:O_HGN + 256] * _silu(g_ref[...])

    acc = _dot(ya.astype(BF16), w_ref[0:512, :])
    acc = acc + _dot(ob_ref[...].astype(BF16), w_ref[512:768, :])
    acc = acc + _dot(yc.astype(BF16), w_ref[768:1024, :])
    acc = acc + _dot(od_ref[...].astype(BF16), w_ref[1024:1280, :])
    o_ref[...] = x + m[5:6] * acc


def _out_proj(xs, modsel, oa, ob, od, oc, zc, w, pvec, seg64, *, n_lat_tiles, n_rows):
    bsz, t, _ = xs.shape
    n_tiles = pl.cdiv(n_rows, ROW_TILE)
    row = lambda width: pl.BlockSpec((None, ROW_TILE, width), lambda b, i: (b, i, 0))
    return pl.pallas_call(
        _out_kernel,
        grid=(bsz, n_tiles),
        in_specs=[
            row(D_MODEL),
            pl.BlockSpec((None, None, N_MOD, D_MODEL),
                         lambda b, i: (b, jnp.where(i >= n_lat_tiles, 1, 0), 0, 0)),
            row(512), row(256), row(256),
            pl.BlockSpec((None, None, ROW_TILE, GROUP_WIDTH), lambda b, i: (b, 0, i, 0)),
            pl.BlockSpec((None, None, ROW_TILE, GROUP_WIDTH), lambda b, i: (b, 1, i, 0)),
            pl.BlockSpec((None, ROW_TILE, GROUP_WIDTH), lambda b, i: (b, i, C_COLS // GROUP_WIDTH - 1)),
            _resident((1280, D_MODEL)),
            _resident((1, O_LEN)),
            _resident((256, 256)),
        ],
        out_specs=row(D_MODEL),
        out_shape=jax.ShapeDtypeStruct((bsz, t, D_MODEL), F32),
        compiler_params=_cparams(("parallel", "parallel")),
        name="mix_out_proj",
    )(xs, modsel, oa, ob, od, oc, oc, zc, w, pvec, seg64)


def _block_diag_ones(n, size):
    idx = np.arange(n) // size
    return jnp.asarray((idx[:, None] == idx[None, :]).astype(np.float32), dtype=BF16)


def _rope_table(n_lat, n_ctx):
    rows = n_lat // GRID_W
    r = jnp.repeat(jnp.arange(rows), GRID_W).astype(F32)
    col = jnp.tile(jnp.arange(GRID_W), rows).astype(F32)

    def pattern(rot_dim):
        n_freq = rot_dim // 4
        inv = ROPE_THETA ** (-jnp.arange(n_freq, dtype=F32) / n_freq)
        ang = jnp.concatenate([r[:, None] * inv, col[:, None] * inv], axis=-1)
        cos, sin = jnp.cos(ang), jnp.sin(ang)
        cos_l = jnp.tile(jnp.concatenate([cos, cos], axis=-1), (1, LANES // rot_dim))
        sin_l = jnp.tile(jnp.concatenate([-sin, sin], axis=-1), (1, LANES // rot_dim))
        cos_l = jnp.concatenate([cos_l, jnp.ones((n_ctx, LANES), F32)], axis=0)
        sin_l = jnp.concatenate([sin_l, jnp.zeros((n_ctx, LANES), F32)], axis=0)
        return cos_l, sin_l

    c32, s32 = pattern(32)
    c64, s64 = pattern(64)
    return jnp.concatenate([c32, s32, c64, s64], axis=1)


def _tile_lanes(v, reps):
    return jnp.tile(v.astype(F32), reps)


def kernel(x, c, ctx, c_ctx, ada_w, ada_b, norm_w, ffn_w_in, ffn_w_out, mix_w_in, mix_w_out, da_qk_norm, da_lambda,
           da_subln, mla_q_norm, mla_kv_norm, mla_w_uq, mla_w_ukv, mla_nope_norm, mla_rope_norm, hg_lb_logits,
           hg_out_norm, gq_qk_norm):
    bsz, n_lat, _ = x.shape
    n_ctx = ctx.shape[1]
    depth = ada_w.shape[0]
    assert bsz + 1 <= 8 and n_lat % ATT_TK == 0 and n_lat % ATT_TQ == 0 and n_lat % n_ctx == 0
    assert n_lat % ROW_TILE == 0 and n_ctx % HG_BLOCK == 0
    t = n_lat + n_ctx
    n_lat_tiles = n_lat // ROW_TILE

    xs = jnp.concatenate([x, ctx], axis=1)

    cc = jnp.zeros((8, D_MODEL), F32).at[0:bsz].set(c).at[bsz].set(c_ctx)
    mods = _modulation(cc, ada_w, ada_b).reshape(depth, 8, N_MOD, D_MODEL)

    p = jax.nn.softmax(hg_lb_logits.astype(F32), axis=1)
    lbs = jnp.maximum(jnp.cumsum(p, axis=1) - p[:, :1], 0.0)

    rope = _rope_table(n_lat, n_ctx)
    seg32 = _block_diag_ones(256, 32)
    seg64 = _block_diag_ones(256, 64)
    ones_sub = _block_diag_ones(HG_BLOCK, HG_SUB)
    ii = np.arange(HG_BLOCK)
    same = (ii[:, None] // HG_SUB) == (ii[None, :] // HG_SUB)
    tri = jnp.asarray(np.stack([same & (ii[None, :] <= ii[:, None]), same & (ii[None, :] >= ii[:, None])])
                      .astype(np.float32), dtype=BF16)

    def group_cols(w, first):
        r = w.shape[0]
        w3 = w.reshape(r, MLA_HEADS, -1)
        return jnp.concatenate([w3[:, :, :first].reshape(r, -1), w3[:, :, first:].reshape(r, -1)], axis=1)

    def fast_flag(qn2, kn2, scale):
        bound = jnp.sqrt(qn2 * kn2) * (scale * LOG2E * 1.02)
        return (bound <= FAST_SCORE_BOUND).astype(jnp.int32).reshape(1)

    def gmax2(g):
        return jnp.max(jnp.abs(g.astype(F32))) ** 2

    for l in range(depth):
        modsel = jnp.stack([mods[l, 0:bsz], jnp.broadcast_to(mods[l, bsz], (bsz, N_MOD, D_MODEL))], axis=1)
        nw = norm_w[l].astype(F32)
        last = l == depth - 1

        xs = _ffn(xs, modsel, nw, ffn_w_in[l, 0].astype(BF16), ffn_w_out[l, 0].astype(BF16),
                  sub=0, n_lat_tiles=n_lat_tiles, n_rows=t)

        wl = mix_w_in[l]
        w_in = jnp.concatenate(
            [wl[:, 0:A_COLS], wl[:, A_COLS:A_COLS + B_COLS], jnp.zeros((D_MODEL, B_PAD - B_COLS), F32),
             wl[:, A_COLS + B_COLS:]], axis=1).astype(BF16)
        pvec = jnp.concatenate([
            _tile_lanes(da_qk_norm[l, 0], 8), _tile_lanes(da_qk_norm[l, 1], 8),
            mla_q_norm[l].astype(F32), mla_kv_norm[l].astype(F32),
            _tile_lanes(mla_nope_norm[l, 0], 4), _tile_lanes(mla_nope_norm[l, 1], 4),
            _tile_lanes(mla_rope_norm[l, 0], 4), _tile_lanes(mla_rope_norm[l, 1], 4),
            _tile_lanes(gq_qk_norm[l, 0], 4), _tile_lanes(gq_qk_norm[l, 1], 2)]).reshape(1, P_LEN)
        qa, ka, va, qb, kb, vb, qd, kd, vd, zc = _prep(
            xs, modsel, nw, w_in, rope, seg32, seg64, pvec,
            group_cols(mla_w_uq[l], MLA_NOPE_DIM).astype(BF16), group_cols(mla_w_ukv[l], MLA_NOPE_DIM).astype(BF16),
            n_lat_tiles=n_lat_tiles)

        fa = fast_flag(DA_QK_DIM * gmax2(da_qk_norm[l, 0]), DA_QK_DIM * gmax2(da_qk_norm[l, 1]), DA_QK_DIM ** -0.5)
        fb = fast_flag(MLA_NOPE_DIM * gmax2(mla_nope_norm[l, 0]) + MLA_ROPE_DIM * gmax2(mla_rope_norm[l, 0]),
                       MLA_NOPE_DIM * gmax2(mla_nope_norm[l, 1]) + MLA_ROPE_DIM * gmax2(mla_rope_norm[l, 1]),
                       (MLA_NOPE_DIM + MLA_ROPE_DIM) ** -0.5)
        fd = fast_flag(HEAD_DIM * gmax2(gq_qk_norm[l, 0]), HEAD_DIM * gmax2(gq_qk_norm[l, 1]), HEAD_DIM ** -0.5)
        oa = _attend(fa, qa, ka, va, n_lat=n_lat, n_ctx=n_ctx, need_ctx=not last)
        ob = _attend(fb, qb, kb, vb, n_lat=n_lat, n_ctx=n_ctx, need_ctx=not last)
        od = _attend(fd, qd, kd, vd, n_lat=n_lat, n_ctx=n_ctx, need_ctx=not last)

        lb = lbs[:, l]
        lbp = jnp.zeros((2, 8, HG_FDIM), F32)
        lbp = lbp.at[:, 0].set(jnp.log(lb)).at[:, 1].set(jnp.log1p(-lb)).at[:, 2].set(1.0 - lb)
        oc = _hgrn(zc, lbp, tri, ones_sub, n_lat=n_lat, n_ctx=n_ctx)

        lam_init = 0.8 - 0.6 * math.exp(-0.3 * l)
        lp = da_lambda[l].astype(F32)
        lam = jnp.exp(jnp.sum(lp[0] * lp[1])) - jnp.exp(jnp.sum(lp[2] * lp[3])) + lam_init
        subln = jnp.concatenate([da_subln[l].astype(F32) * (1.0 - lam_init), jnp.zeros((HEAD_DIM,), F32)])
        opv = jnp.concatenate([jnp.tile(subln, DA_HEADS), _tile_lanes(hg_out_norm[l], HG_HEADS),
                               jnp.full((LANES,), lam, F32)]).reshape(1, O_LEN)
        wo = mix_w_out[l]
        wa = jnp.concatenate([wo[0:GROUP_WIDTH].reshape(DA_HEADS, HEAD_DIM, D_MODEL),
                              jnp.zeros((DA_HEADS, HEAD_DIM, D_MODEL), F32)], axis=1).reshape(2 * GROUP_WIDTH, D_MODEL)
        w_out = jnp.concatenate([wa, wo[GROUP_WIDTH:]], axis=0).astype(BF16)
        xs = _out_proj(xs, modsel, oa, ob, od, oc, zc, w_out, opv, seg64, n_lat_tiles=n_lat_tiles,
                       n_rows=n_lat if last else t)

        xs = _ffn(xs, modsel, nw, ffn_w_in[l, 1].astype(BF16), ffn_w_out[l, 1].astype(BF16),
                  sub=2, n_lat_tiles=n_lat_tiles, n_rows=n_lat if last else t)
    return xs
```

```python
import functools
import math

import numpy as np
import jax
import jax.numpy as jnp
from jax import lax
from jax.experimental import pallas as pl
from jax.experimental.pallas import tpu as pltpu

F32 = jnp.float32
BF16 = jnp.bfloat16

D_MODEL = 1024
HEAD_DIM = 64
GROUP_WIDTH = D_MODEL // 4
EPS = 1e-6
ROPE_THETA = 10000.0
GRID_W = 64
HALF = 0.5
N_MOD = 9
D_FF = 2816

DA_HEADS = 4
DA_QK_DIM = 32
MLA_HEADS = 4
MLA_Q_RANK = 256
MLA_KV_RANK = 128
MLA_NOPE_DIM = 64
MLA_ROPE_DIM = 32
HG_HEADS = 4
HG_EXPAND = 128
HG_FDIM = 512
HG_VDIM = 64
GQ_HEADS = 4
GQ_KV_HEADS = 2

A_COLS = 768
B_COLS = 416
C_COLS = 2048
D_COLS = 512
B_PAD = 512
W_COLS = A_COLS + B_PAD + C_COLS + D_COLS

LANES = 128
ROW_TILE = 512
ATT_TQ = 2048
ATT_HEADS = 4
ATT_TK_MAX = 2048
V_ROWS = HEAD_DIM
LOG2E = 1.4426950408889634
FAST_SCORE_BOUND = 64.0
HG_BLOCK = 256
HG_SUB = 16
HG_TGROUP = 8
FF_CHUNK = 512
VMEM_LIMIT = 56 * 1024 * 1024

P_GAQ, P_GAK, P_QN, P_KVN, P_NOPE0, P_NOPE1, P_ROPE0, P_ROPE1, P_GDQ, P_GDK, P_LEN = (
    0, 256, 512, 768, 896, 1152, 1408, 1536, 1664, 1920, 2048)
O_SUBLN, O_HGN, O_LAM, O_LEN = 0, 512, 768, 896


def _cparams(sem):
    return pltpu.CompilerParams(dimension_semantics=sem, vmem_limit_bytes=VMEM_LIMIT)


def _resident(shape):
    nd = len(shape)
    return pl.BlockSpec(shape, lambda *_: (0,) * nd, pipeline_mode=pl.Buffered(1))


def _dot(a, b):
    return jnp.dot(a, b, preferred_element_type=F32)


def _rms_rows(x):
    return x * lax.rsqrt(jnp.mean(x * x, axis=-1, keepdims=True) + EPS)


def _seg_norm(x, seg, size):
    x2 = x * x
    hi = x2.astype(BF16)
    lo = (x2 - hi.astype(F32)).astype(BF16)
    ss = _dot(hi, seg) + _dot(lo, seg)
    return x * lax.rsqrt(ss * (1.0 / size) + EPS)


def _swap_halves(x, half):
    lane = lax.broadcasted_iota(jnp.int32, x.shape, 1)
    up = pltpu.roll(x, LANES - half, 1)
    down = pltpu.roll(x, half, 1)
    return jnp.where((lane % (2 * half)) < half, up, down)


def _rope(x, cos, sin, half):
    parts = []
    for c in range(x.shape[1] // LANES):
        xs = x[:, c * LANES:(c + 1) * LANES]
        parts.append(xs * cos + _swap_halves(xs, half) * sin)
    return parts[0] if len(parts) == 1 else jnp.concatenate(parts, axis=1)


def _silu(x):
    return x / (1.0 + jnp.exp(-x))


def _mod_kernel(c_ref, w_ref, b_ref, o_ref):
    a = _silu(c_ref[...])
    a_hi = a.astype(BF16)
    a_lo = (a - a_hi.astype(F32)).astype(BF16)
    w = w_ref[...]
    w_hi = w.astype(BF16)
    w_lo = (w - w_hi.astype(F32)).astype(BF16)
    o_ref[...] = _dot(a_hi, w_hi) + _dot(a_hi, w_lo) + _dot(a_lo, w_hi) + b_ref[...]


def _modulation(cc, ada_w, ada_b):
    depth = ada_w.shape[0]
    nb = N_MOD * D_MODEL // D_MODEL
    return pl.pallas_call(
        _mod_kernel,
        grid=(depth, nb),
        in_specs=[
            pl.BlockSpec((8, D_MODEL), lambda l, j: (0, 0)),
            pl.BlockSpec((None, D_MODEL, D_MODEL), lambda l, j: (l, 0, j)),
            pl.BlockSpec((None, 1, D_MODEL), lambda l, j: (l, 0, j)),
        ],
        out_specs=pl.BlockSpec((None, 8, D_MODEL), lambda l, j: (l, 0, j)),
        out_shape=jax.ShapeDtypeStruct((depth, 8, N_MOD * D_MODEL), F32),
        compiler_params=_cparams(("parallel", "parallel")),
        name="adaln_mod",
    )(cc, ada_w, ada_b.reshape(depth, 1, N_MOD * D_MODEL))


def _ffn_rows(x, m, nw_ref, win_ref, wout_ref, sub):
    shift, scale, gate = m[3 * sub:3 * sub + 1], m[3 * sub + 1:3 * sub + 2], m[3 * sub + 2:3 * sub + 3]
    h = (_rms_rows(x) * nw_ref[sub:sub + 1]) * (1.0 + scale) + shift
    hb = h.astype(BF16)
    acc = jnp.zeros(x.shape, F32)
    for lo in range(0, D_FF, FF_CHUNK):
        hi = min(lo + FF_CHUNK, D_FF)
        g = _dot(hb, win_ref[:, lo:hi])
        u = _dot(hb, win_ref[:, D_FF + lo:D_FF + hi])
        a = (_silu(g) * u).astype(BF16)
        acc = acc + _dot(a, wout_ref[lo:hi, :])
    return x + HALF * gate * acc


def _ffn_kernel(x_ref, c_ref, mod_ref, nw_ref, win_ref, wout_ref, o_ref, *, sub, n_lat_tiles):
    x = x_ref[...]
    if c_ref is not None:
        x = jnp.where(pl.program_id(1) >= n_lat_tiles, c_ref[...], x)
    o_ref[...] = _ffn_rows(x, mod_ref[...], nw_ref, win_ref, wout_ref, sub)


def _ffn(xs, ctx, modsel, nw, win, wout, *, sub, n_lat_tiles, n_rows):
    bsz = xs.shape[0]
    if ctx is None:
        kern = lambda x_ref, *rest: _ffn_kernel(x_ref, None, *rest, sub=sub, n_lat_tiles=n_lat_tiles)
        x_specs = [pl.BlockSpec((None, ROW_TILE, D_MODEL), lambda b, i: (b, i, 0))]
        x_args = [xs]
    else:
        kern = functools.partial(_ffn_kernel, sub=sub, n_lat_tiles=n_lat_tiles)
        x_specs = [pl.BlockSpec((None, ROW_TILE, D_MODEL), lambda b, i: (b, jnp.minimum(i, n_lat_tiles - 1), 0)),
                   pl.BlockSpec((None, ROW_TILE, D_MODEL), lambda b, i: (b, jnp.maximum(i - n_lat_tiles, 0), 0))]
        x_args = [xs, ctx]
    return pl.pallas_call(
        kern,
        grid=(bsz, pl.cdiv(n_rows, ROW_TILE)),
        in_specs=x_specs + [
            pl.BlockSpec((None, None, N_MOD, D_MODEL),
                         lambda b, i: (b, jnp.where(i >= n_lat_tiles, 1, 0), 0, 0)),
            _resident((3, D_MODEL)),
            _resident((D_MODEL, 2 * D_FF)),
            _resident((D_FF, D_MODEL)),
        ],
        out_specs=pl.BlockSpec((None, ROW_TILE, D_MODEL), lambda b, i: (b, i, 0)),
        out_shape=jax.ShapeDtypeStruct((bsz, n_rows, D_MODEL), F32),
        compiler_params=_cparams(("parallel", "parallel")),
        name=f"ffn_half_{sub}",
    )(*x_args, modsel, nw, win, wout)


def _prep_kernel(x_ref, mod_ref, nw_ref, w_ref, rope_ref, seg32_ref, seg64_ref, pv_ref, wuq_ref, wukv_ref,
                 qa_ref, ka_ref, va_ref, qb_ref, kb_ref, vb_ref, qd_ref, kd_ref, vd_ref, zc_ref):
    x = x_ref[...]
    m = mod_ref[...]
    h = (_rms_rows(x) * nw_ref[1:2]) * (1.0 + m[4:5]) + m[3:4]
    hb = h.astype(BF16)
    seg32 = seg32_ref[...]
    seg64 = seg64_ref[...]
    cos32, sin32 = rope_ref[:, 0:128], rope_ref[:, 128:256]
    cos64, sin64 = rope_ref[:, 256:384], rope_ref[:, 384:512]

    def pv(lo, hi):
        return pv_ref[:, lo:hi]

    za = _dot(hb, w_ref[:, 0:A_COLS])
    qa = _rope(_seg_norm(za[:, 0:256], seg32, 32) * pv(P_GAQ, P_GAQ + 256), cos32, sin32, 16)
    qa = qa * (DA_QK_DIM ** -0.5 * LOG2E)
    ka = _rope(_seg_norm(za[:, 256:512], seg32, 32) * pv(P_GAK, P_GAK + 256), cos32, sin32, 16)
    qa_t = jnp.transpose(qa)
    for hh in range(2 * DA_HEADS):
        qa_ref[hh] = qa_t[32 * hh:32 * hh + 32, :].astype(BF16)
        ka_ref[hh] = ka[:, 32 * hh:32 * hh + 32].astype(BF16)

    def put_v(ref, hh, val_t):
        ref[hh] = val_t.astype(BF16)

    va_t = jnp.transpose(za[:, 512:768])
    for hh in range(DA_HEADS):
        put_v(va_ref, hh, va_t[64 * hh:64 * hh + 64, :])

    zb = _dot(hb, w_ref[:, A_COLS:A_COLS + B_PAD])
    cq = _rms_rows(zb[:, 0:256]) * pv(P_QN, P_QN + 256)
    q2 = _dot(cq.astype(BF16), wuq_ref[...])
    ckv = _rms_rows(zb[:, 256:384]) * pv(P_KVN, P_KVN + 128)
    kv2 = _dot(ckv.astype(BF16), wukv_ref[...])
    sb = (MLA_NOPE_DIM + MLA_ROPE_DIM) ** -0.5 * LOG2E
    qn = _seg_norm(q2[:, 0:256], seg64, 64) * pv(P_NOPE0, P_NOPE0 + 256) * sb
    kn = _seg_norm(kv2[:, 0:256], seg64, 64) * pv(P_NOPE1, P_NOPE1 + 256)
    seg32s = seg32[0:128, 0:128]
    qr = _rope(_seg_norm(q2[:, 256:384], seg32s, 32) * pv(P_ROPE0, P_ROPE0 + 128), cos32, sin32, 16) * sb
    kr = _rope(_seg_norm(zb[:, 384:512], seg32s, 32) * pv(P_ROPE1, P_ROPE1 + 128), cos32, sin32, 16)
    zpad = jnp.zeros((x.shape[0], 32), BF16)
    zpad_t = jnp.zeros((32, x.shape[0]), BF16)
    qn_t = jnp.transpose(qn)
    qr_t = jnp.transpose(qr)
    vb_t = jnp.transpose(kv2[:, 256:512])
    for hh in range(MLA_HEADS):
        qb_ref[hh, 0:64, :] = qn_t[64 * hh:64 * hh + 64, :].astype(BF16)
        qb_ref[hh, 64:96, :] = qr_t[32 * hh:32 * hh + 32, :].astype(BF16)
        qb_ref[hh, 96:128, :] = zpad_t
        kb_ref[hh, :, 0:64] = kn[:, 64 * hh:64 * hh + 64].astype(BF16)
        kb_ref[hh, :, 64:96] = kr[:, 0:32].astype(BF16)
        kb_ref[hh, :, 96:128] = zpad
        put_v(vb_ref, hh, vb_t[64 * hh:64 * hh + 64, :])

    zc_ref[...] = _dot(hb, w_ref[:, A_COLS + B_PAD:A_COLS + B_PAD + C_COLS])

    zd = _dot(hb, w_ref[:, A_COLS + B_PAD + C_COLS:W_COLS])
    qd = _rope(_seg_norm(zd[:, 0:256], seg64, 64) * pv(P_GDQ, P_GDQ + 256), cos64, sin64, 32)
    qd = qd * (HEAD_DIM ** -0.5 * LOG2E)
    kd = _rope(_seg_norm(zd[:, 256:384], seg64[0:128, 0:128], 64) * pv(P_GDK, P_GDK + 128), cos64, sin64, 32)
    qd_t = jnp.transpose(qd)
    vd_t = jnp.transpose(zd[:, 384:512])
    for hh in range(GQ_HEADS):
        qd_ref[hh] = qd_t[64 * hh:64 * hh + 64, :].astype(BF16)
    for hh in range(GQ_KV_HEADS):
        kd_ref[hh] = kd[:, 64 * hh:64 * hh + 64].astype(BF16)
        put_v(vd_ref, hh, vd_t[64 * hh:64 * hh + 64, :])


def _prep(xs, modsel, nw, w, rope, seg32, seg64, pvec, wuq, wukv, *, n_lat_tiles):
    bsz, t, _ = xs.shape
    n_tiles = pl.cdiv(t, ROW_TILE)

    def heads(h, d):
        return (pl.BlockSpec((None, h, ROW_TILE, d), lambda b, i: (b, 0, i, 0)),
                jax.ShapeDtypeStruct((bsz, h, t, d), BF16))

    def heads_t(h, d):
        return (pl.BlockSpec((None, h, d, ROW_TILE), lambda b, i: (b, 0, 0, i)),
                jax.ShapeDtypeStruct((bsz, h, d, t), BF16))

    outs = [heads_t(8, 32), heads(8, 32), heads_t(4, V_ROWS),
            heads_t(4, 128), heads(4, 128), heads_t(4, V_ROWS),
            heads_t(4, 64), heads(2, 64), heads_t(2, V_ROWS),
            (pl.BlockSpec((None, ROW_TILE, C_COLS), lambda b, i: (b, i, 0)),
             jax.ShapeDtypeStruct((bsz, t, C_COLS), F32))]
    return pl.pallas_call(
        _prep_kernel,
        grid=(bsz, n_tiles),
        in_specs=[
            pl.BlockSpec((None, ROW_TILE, D_MODEL), lambda b, i: (b, i, 0)),
            pl.BlockSpec((None, None, N_MOD, D_MODEL),
                         lambda b, i: (b, jnp.where(i >= n_lat_tiles, 1, 0), 0, 0)),
            _resident((3, D_MODEL)),
            _resident((D_MODEL, W_COLS)),
            pl.BlockSpec((ROW_TILE, 512), lambda b, i: (i, 0)),
            _resident((256, 256)),
            _resident((256, 256)),
            _resident((1, P_LEN)),
            _resident((MLA_Q_RANK, 384)),
            _resident((MLA_KV_RANK, 512)),
        ],
        out_specs=[o[0] for o in outs],
        out_shape=[o[1] for o in outs],
        compiler_params=_cparams(("parallel", "parallel")),
        name="mix_in_prep",
    )(xs, modsel, nw, w, rope, seg32, seg64, pvec, wuq, wukv)


def _attn_kernel(flag_ref, q_ref, k_ref, v_ref, kc_ref, vc_ref, o_ref, acc_scr, l_scr, m_scr, *, hq, hk, hv, n_kv):
    j = pl.program_id(3)
    fast = flag_ref[0] == 1

    @pl.when(j == 0)
    def _():
        acc_scr[...] = jnp.zeros(acc_scr.shape, F32)
        l_scr[...] = jnp.zeros(l_scr.shape, F32)
        m_scr[...] = jnp.full(m_scr.shape, -jnp.inf, F32)

    def scores(h, kr):
        return _dot(kr[h // (hq // hk)], q_ref[h])

    def step_fast(kr, vr):
        for h in range(hq):
            p = jnp.exp2(scores(h, kr))
            l_scr[h] += jnp.sum(p, axis=0, keepdims=True)
            acc_scr[h] += _dot(vr[h // (hq // hv)], p.astype(BF16))

    def step_safe(kr, vr):
        for h in range(hq):
            s = scores(h, kr)
            m_prev = m_scr[h]
            m_new = jnp.maximum(m_prev, jnp.max(s, axis=0, keepdims=True))
            alpha = jnp.exp2(m_prev - m_new)
            p = jnp.exp2(s - m_new[0:1, :])
            l_scr[h] = alpha * l_scr[h] + jnp.sum(p, axis=0, keepdims=True)
            acc_scr[h] = alpha[0:1, :] * acc_scr[h] + _dot(vr[h // (hq // hv)], p.astype(BF16))
            m_scr[h] = m_new

    def both(kr, vr):
        @pl.when(fast)
        def _():
            step_fast(kr, vr)

        @pl.when(jnp.logical_not(fast))
        def _():
            step_safe(kr, vr)

    @pl.when(j < n_kv)
    def _():
        both(k_ref, v_ref)

    @pl.when(j == n_kv)
    def _():
        both(kc_ref, vc_ref)
        outs = []
        for h in range(hq):
            outs.append(acc_scr[h] / l_scr[h][0:1, :])
        o_ref[...] = jnp.transpose(jnp.concatenate(outs, axis=0))


def _att_key_tile(hq, n_lat):
    tk = ATT_TK_MAX
    while tk > LANES and (hq * ATT_TQ * tk * 6 > VMEM_LIMIT or n_lat % tk):
        tk //= 2
    return tk


def _attention(flag, q, k, v, prev, *, n_lat, n_ctx, latent):
    bsz, hq_all, dq, t = q.shape
    n_grp = hq_all // ATT_HEADS
    hq = ATT_HEADS
    hk, hv = k.shape[1] // n_grp, v.shape[1] // n_grp
    width = hq * HEAD_DIM
    ctx_blk = n_lat // n_ctx
    tk = _att_key_tile(hq, n_lat)
    if latent:
        tq, nq, n_kv, q_off = ATT_TQ, n_lat // ATT_TQ, n_lat // tk, 0
    else:
        tq, nq, n_kv, q_off = n_ctx, 1, 0, ctx_blk

    def kv_blk(j):
        return jnp.minimum(j, max(n_kv - 1, 0))

    in_specs = [
        pl.BlockSpec((None, hq, dq, tq), lambda b, g, i, j, f: (b, g, 0, i + q_off)),
        pl.BlockSpec((None, hk, tk, dq), lambda b, g, i, j, f: (b, g, kv_blk(j), 0)),
        pl.BlockSpec((None, hv, V_ROWS, tk), lambda b, g, i, j, f: (b, g, 0, kv_blk(j))),
        pl.BlockSpec((None, hk, n_ctx, dq), lambda b, g, i, j, f: (b, g, ctx_blk, 0)),
        pl.BlockSpec((None, hv, V_ROWS, n_ctx), lambda b, g, i, j, f: (b, g, 0, ctx_blk)),
    ]
    args = [flag, q, k, v, k, v]
    aliases = {}
    if not latent:
        in_specs.append(pl.BlockSpec(memory_space=pl.ANY))
        args.append(prev)
        aliases = {6: 0}

    def body(flag_ref, q_ref, k_ref, v_ref, kc_ref, vc_ref, *rest):
        o_ref, acc_scr, l_scr, m_scr = rest[-4:]
        _attn_kernel(flag_ref, q_ref, k_ref, v_ref, kc_ref, vc_ref, o_ref, acc_scr, l_scr, m_scr,
                     hq=hq, hk=hk, hv=hv, n_kv=n_kv)

    grid_spec = pltpu.PrefetchScalarGridSpec(
        num_scalar_prefetch=1,
        grid=(bsz, n_grp, nq, n_kv + 1),
        in_specs=in_specs,
        out_specs=pl.BlockSpec((None, tq, width), lambda b, g, i, j, f: (b, i + q_off, g)),
        scratch_shapes=[pltpu.VMEM((hq, HEAD_DIM, tq), F32), pltpu.VMEM((hq, 8, tq), F32),
                        pltpu.VMEM((hq, 8, tq), F32)],
    )
    return pl.pallas_call(
        body,
        grid_spec=grid_spec,
        out_shape=jax.ShapeDtypeStruct((bsz, t, hq_all * HEAD_DIM), F32),
        input_output_aliases=aliases,
        compiler_params=_cparams(("parallel", "parallel", "parallel", "arbitrary")),
        name=f"flash_attn_h{hq_all}_d{dq}_{'lat' if latent else 'ctx'}",
    )(*args)


def _attend(flag, q, k, v, *, n_lat, n_ctx, need_ctx):
    o = _attention(flag, q, k, v, None, n_lat=n_lat, n_ctx=n_ctx, latent=True)
    if need_ctx:
        o = _attention(flag, q, k, v, o, n_lat=n_lat, n_ctx=n_ctx, latent=False)
    return o


def _hgrn_block(zq_ref, zx_ref, zi_ref, lb_ref, tri_ref, negm_ref, ones_ref, segh_ref, segg_ref, o_ref, st_scr, oi_scr,
                *, rev):
    ng = HG_BLOCK // HG_SUB
    x2 = zx_ref[...] * LOG2E
    lb = lb_ref[...]
    log_lb, log_1m, one_m = lb[0:1], lb[1:2], lb[2:3]
    e = jnp.exp2(-jnp.abs(x2))
    rcp = 1.0 / (1.0 + e)
    log_sig = jnp.minimum(x2, 0.0) - jnp.log2(1.0 + e)
    b = log_1m + log_sig
    big = jnp.maximum(log_lb, b)
    logf = big + jnp.log2(1.0 + jnp.exp2(-jnp.abs(log_lb - b)))
    kk = one_m * (jnp.where(x2 >= 0.0, e, 1.0) * rcp)
    zq = zq_ref[...]
    q = zq / (1.0 + jnp.exp2(zq * -LOG2E))
    v = zi_ref[:, 0:GROUP_WIDTH]

    h1 = logf.astype(BF16)
    r1 = logf - h1.astype(F32)
    h2 = r1.astype(BF16)
    h3 = (r1 - h2.astype(F32)).astype(BF16)
    tri = tri_ref[...]
    ones = ones_ref[...]
    g = _dot(tri, h1) + _dot(tri, h2) + _dot(tri, h3)
    gtot = _dot(ones, h1) + _dot(ones, h2) + _dot(ones, h3)
    qdec = jnp.exp2(g)
    kdec = jnp.exp2(gtot - g)
    dec = qdec * kdec
    qtb = (q * qdec).astype(BF16)
    ktb = (kk * kdec).astype(BF16)
    vb = v.astype(BF16)

    heads = [(slice(HG_EXPAND * hh, HG_EXPAND * (hh + 1)), slice(HG_VDIM * hh, HG_VDIM * (hh + 1)))
             for hh in range(HG_HEADS)]
    upd = []
    for n in range(ng):
        r = slice(n * HG_SUB, (n + 1) * HG_SUB)
        upd.append(jnp.concatenate(
            [lax.dot_general(vb[r, vs], ktb[r, ks], (((0,), (0,)), ((), ())), preferred_element_type=F32)
             for ks, vs in heads], axis=1))
    st = st_scr[...]
    inter = [None] * ng
    for n in (range(ng - 1, -1, -1) if rev else range(ng)):
        r = slice(n * HG_SUB, (n + 1) * HG_SUB)
        stb = st.astype(BF16)
        inter[n] = jnp.concatenate(
            [lax.dot_general(qtb[r, ks], stb[:, ks], (((1,), (1,)), ((), ())), preferred_element_type=F32)
             for ks, _ in heads], axis=1)
        st = st * dec[n * HG_SUB:n * HG_SUB + 1, :] + upd[n]
    st_scr[...] = st

    g3 = g.reshape(ng, HG_SUB, HG_FDIM)
    k3 = kk.reshape(ng, HG_SUB, HG_FDIM)
    q3 = q.reshape(ng, HG_SUB, HG_FDIM)
    v3 = v.reshape(ng, HG_SUB, GROUP_WIDTH)
    segh = segh_ref[...]
    half = HG_SUB // 2

    def span(t):
        if rev:
            return (half, HG_SUB) if t >= half else (0, HG_SUB)
        return (0, half) if t < half else (0, HG_SUB)

    for t0 in range(0, HG_SUB, HG_TGROUP):
        ts = range(t0, t0 + HG_TGROUP)
        wbs = []
        for t in ts:
            s0, s1 = span(t)
            nm = negm_ref[t, s0:s1, :]
            nm = jnp.concatenate([nm] * (HG_FDIM // LANES), axis=1)[None]
            w = jnp.exp2(g3[:, t:t + 1, :] - g3[:, s0:s1, :] + nm) * k3[:, s0:s1, :] * q3[:, t:t + 1, :]
            wbs.append(w.reshape(ng * (s1 - s0), HG_FDIM).astype(BF16))
        wb = jnp.concatenate(wbs, axis=0)
        hr = wb.shape[0] // 2
        cexp = jnp.concatenate([_dot(wb[0:hr], segh), _dot(wb[hr:], segh)], axis=0)
        row = 0
        for t in ts:
            s0, s1 = span(t)
            nr = ng * (s1 - s0)
            prod = (cexp[row:row + nr] * v3[:, s0:s1, :].reshape(nr, GROUP_WIDTH)).astype(BF16)
            row += nr
            segg = segg_ref[0 if s1 - s0 == half else 1]
            res = _dot(segg[:, 0:nr], prod)
            for c in range(GROUP_WIDTH // LANES):
                oi_scr[c][pl.ds(t, ng, stride=HG_SUB), :] = res[:, c * LANES:(c + 1) * LANES]
    intra = jnp.concatenate([oi_scr[c][...] for c in range(GROUP_WIDTH // LANES)], axis=1)
    o_ref[...] = jnp.concatenate(inter, axis=0) + intra


def _hgrn_kernel(zq_ref, zx_ref, zi_ref, lb_ref, tri_ref, negm_ref, ones_ref, segh_ref, segg_ref, o_ref, st_scr,
                 *oi_scr):
    d = pl.program_id(1)
    j = pl.program_id(2)

    @pl.when(j == 0)
    def _():
        st_scr[...] = jnp.zeros(st_scr.shape, F32)

    args = (zq_ref, zx_ref, zi_ref, lb_ref, tri_ref, negm_ref, ones_ref, segh_ref, segg_ref, o_ref, st_scr, oi_scr)

    @pl.when(d == 0)
    def _():
        _hgrn_block(*args, rev=False)

    @pl.when(d == 1)
    def _():
        _hgrn_block(*args, rev=True)


def _hgrn(zc, lbp, tri, negm, ones, segh, segg, *, n_lat, n_ctx):
    bsz, t, _ = zc.shape
    nl = n_lat // HG_BLOCK
    nc = n_ctx // HG_BLOCK

    def blk(d, j):
        ctx = jnp.where(d == 0, nl + j, nl + nc - 1 - j)
        lat = jnp.where(d == 0, j - nc, nl - 1 - (j - nc))
        return jnp.where(j < nc, ctx, lat)

    return pl.pallas_call(
        _hgrn_kernel,
        grid=(bsz, 2, nl + nc),
        in_specs=[
            pl.BlockSpec((None, HG_BLOCK, HG_FDIM), lambda b, d, j: (b, blk(d, j), 0)),
            pl.BlockSpec((None, HG_BLOCK, HG_FDIM), lambda b, d, j: (b, blk(d, j), 1 + d)),
            pl.BlockSpec((None, HG_BLOCK, HG_FDIM), lambda b, d, j: (b, blk(d, j), 3)),
            pl.BlockSpec((None, 8, HG_FDIM), lambda b, d, j: (d, 0, 0)),
            pl.BlockSpec((None, HG_BLOCK, HG_BLOCK), lambda b, d, j: (d, 0, 0)),
            pl.BlockSpec((None, HG_SUB, HG_SUB, LANES), lambda b, d, j: (d, 0, 0, 0)),
            _resident((HG_BLOCK, HG_BLOCK)),
            _resident((HG_FDIM, GROUP_WIDTH)),
            _resident((2, HG_BLOCK // HG_SUB, HG_BLOCK)),
        ],
        out_specs=pl.BlockSpec((None, None, HG_BLOCK, GROUP_WIDTH), lambda b, d, j: (b, d, blk(d, j), 0)),
        out_shape=jax.ShapeDtypeStruct((bsz, 2, t, GROUP_WIDTH), F32),
        scratch_shapes=[pltpu.VMEM((HG_VDIM, HG_FDIM), F32)]
        + [pltpu.VMEM((HG_BLOCK, LANES), F32) for _ in range(GROUP_WIDTH // LANES)],
        compiler_params=_cparams(("parallel", "parallel", "arbitrary")),
        name="hgrn2_scan",
    )(zc, zc, zc, lbp, tri, negm, ones, segh, segg)


def _out_kernel(x_ref, mod_ref, oa_ref, ob_ref, od_ref, ocf_ref, ocb_ref, g_ref, w_ref, pv_ref, seg64_ref, o_ref):
    x = x_ref[...]
    m = mod_ref[...]
    seg64 = seg64_ref[...]
    lam = pv_ref[:, O_LAM:O_LAM + LANES]

    oa = oa_ref[...]
    lane = lax.broadcasted_iota(jnp.int32, (x.shape[0], LANES), 1)
    ya = []
    for c in range(DA_HEADS):
        pair = oa[:, c * LANES:(c + 1) * LANES]
        diff = pair - lam * pltpu.roll(pair, HEAD_DIM, 1)
        ya.append(jnp.where(lane < HEAD_DIM, diff, 0.0))
    ya = jnp.concatenate(ya, axis=1)
    ya = jnp.concatenate([_seg_norm(ya[:, 0:256], seg64, 64), _seg_norm(ya[:, 256:512], seg64, 64)], axis=1)
    ya = ya * pv_ref[:, O_SUBLN:O_SUBLN + 512]

    oc = ocf_ref[...] + ocb_ref[...]
    yc = _seg_norm(oc, seg64, 64) * pv_ref[:, O_HGN:O_HGN + 256] * _silu(g_ref[...])

    acc = _dot(ya.astype(BF16), w_ref[0:512, :])
    acc = acc + _dot(ob_ref[...].astype(BF16), w_ref[512:768, :])
    acc = acc + _dot(yc.astype(BF16), w_ref[768:1024, :])
    acc = acc + _dot(od_ref[...].astype(BF16), w_ref[1024:1280, :])
    o_ref[...] = x + m[5:6] * acc


def _out_proj(xs, modsel, oa, ob, od, oc, zc, w, pvec, seg64, *, n_lat_tiles, n_rows):
    bsz, t, _ = xs.shape
    n_tiles = pl.cdiv(n_rows, ROW_TILE)
    row = lambda width: pl.BlockSpec((None, ROW_TILE, width), lambda b, i: (b, i, 0))
    return pl.pallas_call(
        _out_kernel,
        grid=(bsz, n_tiles),
        in_specs=[
            row(D_MODEL),
            pl.BlockSpec((None, None, N_MOD, D_MODEL),
                         lambda b, i: (b, jnp.where(i >= n_lat_tiles, 1, 0), 0, 0)),
            row(512), row(256), row(256),
            pl.BlockSpec((None, None, ROW_TILE, GROUP_WIDTH), lambda b, i: (b, 0, i, 0)),
            pl.BlockSpec((None, None, ROW_TILE, GROUP_WIDTH), lambda b, i: (b, 1, i, 0)),
            pl.BlockSpec((None, ROW_TILE, GROUP_WIDTH), lambda b, i: (b, i, C_COLS // GROUP_WIDTH - 1)),
            _resident((1280, D_MODEL)),
            _resident((1, O_LEN)),
            _resident((256, 256)),
        ],
        out_specs=row(D_MODEL),
        out_shape=jax.ShapeDtypeStruct((bsz, t, D_MODEL), F32),
        compiler_params=_cparams(("parallel", "parallel")),
        name="mix_out_proj",
    )(xs, modsel, oa, ob, od, oc, oc, zc, w, pvec, seg64)


def _block_diag_ones(n, size):
    idx = np.arange(n) // size
    return jnp.asarray((idx[:, None] == idx[None, :]).astype(np.float32), dtype=BF16)


def _rope_table(n_lat, n_ctx):
    rows = n_lat // GRID_W
    r = jnp.repeat(jnp.arange(rows), GRID_W).astype(F32)
    col = jnp.tile(jnp.arange(GRID_W), rows).astype(F32)

    def pattern(rot_dim):
        n_freq = rot_dim // 4
        inv = ROPE_THETA ** (-jnp.arange(n_freq, dtype=F32) / n_freq)
        ang = jnp.concatenate([r[:, None] * inv, col[:, None] * inv], axis=-1)
        cos, sin = jnp.cos(ang), jnp.sin(ang)
        cos_l = jnp.tile(jnp.concatenate([cos, cos], axis=-1), (1, LANES // rot_dim))
        sin_l = jnp.tile(jnp.concatenate([-sin, sin], axis=-1), (1, LANES // rot_dim))
        cos_l = jnp.concatenate([cos_l, jnp.ones((n_ctx, LANES), F32)], axis=0)
        sin_l = jnp.concatenate([sin_l, jnp.zeros((n_ctx, LANES), F32)], axis=0)
        return cos_l, sin_l

    c32, s32 = pattern(32)
    c64, s64 = pattern(64)
    return jnp.concatenate([c32, s32, c64, s64], axis=1)


def _tile_lanes(v, reps):
    return jnp.tile(v.astype(F32), reps)


def kernel(x, c, ctx, c_ctx, ada_w, ada_b, norm_w, ffn_w_in, ffn_w_out, mix_w_in, mix_w_out, da_qk_norm, da_lambda,
           da_subln, mla_q_norm, mla_kv_norm, mla_w_uq, mla_w_ukv, mla_nope_norm, mla_rope_norm, hg_lb_logits,
           hg_out_norm, gq_qk_norm):
    bsz, n_lat, _ = x.shape
    n_ctx = ctx.shape[1]
    depth = ada_w.shape[0]
    assert bsz + 1 <= 8 and n_lat % ATT_TQ == 0 and n_lat % n_ctx == 0
    assert n_lat % ROW_TILE == 0 and n_ctx % HG_BLOCK == 0
    t = n_lat + n_ctx
    n_lat_tiles = n_lat // ROW_TILE
    xs = x.astype(F32)

    cc =jnp.zeros((8, D_MODEL), F32).at[0:bsz].set(c).at[bsz].set(c_ctx)
    mods = _modulation(cc, ada_w, ada_b).reshape(depth, 8, N_MOD, D_MODEL)

    p = jax.nn.softmax(hg_lb_logits.astype(F32), axis=1)
    lbs = jnp.maximum(jnp.cumsum(p, axis=1) - p[:, :1], 0.0)

    rope = _rope_table(n_lat, n_ctx)
    seg32 = _block_diag_ones(256, 32)
    seg64 = _block_diag_ones(256, 64)
    ones_sub = _block_diag_ones(HG_BLOCK, HG_SUB)
    ii = np.arange(HG_BLOCK)
    same = (ii[:, None] // HG_SUB) == (ii[None, :] // HG_SUB)
    tri = jnp.asarray(np.stack([same & (ii[None, :] <= ii[:, None]), same & (ii[None, :] >= ii[:, None])])
                      .astype(np.float32), dtype=BF16)
    segh = jnp.asarray((np.arange(HG_FDIM)[:, None] // HG_EXPAND == np.arange(GROUP_WIDTH)[None, :] // HG_VDIM)
                       .astype(np.float32), dtype=BF16)
    n_steps = HG_BLOCK // HG_SUB
    segg_np = np.zeros((2, n_steps, HG_BLOCK), np.float32)
    segg_np[0, :, :HG_BLOCK // 2] = np.arange(n_steps)[:, None] == np.arange(HG_BLOCK // 2)[None, :] // (HG_SUB // 2)
    segg_np[1] = np.arange(n_steps)[:, None] == ii[None, :] // HG_SUB
    segg = jnp.asarray(segg_np, dtype=BF16)
    pp = np.arange(HG_SUB)
    negm_np = np.where(np.stack([pp[None, :] <= pp[:, None], pp[None, :] >= pp[:, None]]), 0.0, -np.inf)
    negm = jnp.asarray(np.broadcast_to(negm_np[..., None], (2, HG_SUB, HG_SUB, LANES)).astype(np.float32))

    def group_cols(w, first):
        r = w.shape[0]
        w3 = w.reshape(r, MLA_HEADS, -1)
        return jnp.concatenate([w3[:, :, :first].reshape(r, -1), w3[:, :, first:].reshape(r, -1)], axis=1)

    def fast_flag(qn2, kn2, scale):
        bound = jnp.sqrt(qn2 * kn2) * (scale * LOG2E * 1.02)
        return (bound <= FAST_SCORE_BOUND).astype(jnp.int32).reshape(1)

    def gmax2(g):
        return jnp.max(jnp.abs(g.astype(F32))) ** 2

    for l in range(depth):
        modsel = jnp.stack([mods[l, 0:bsz], jnp.broadcast_to(mods[l, bsz], (bsz, N_MOD, D_MODEL))], axis=1)
        nw = norm_w[l].astype(F32)
        last = l == depth - 1

        xs = _ffn(xs, ctx.astype(F32) if l == 0 else None, modsel, nw,
                  ffn_w_in[l, 0].astype(BF16), ffn_w_out[l, 0].astype(BF16),
                  sub=0, n_lat_tiles=n_lat_tiles, n_rows=t)

        wl = mix_w_in[l]
        w_in = jnp.concatenate(
            [wl[:, 0:A_COLS], wl[:, A_COLS:A_COLS + B_COLS], jnp.zeros((D_MODEL, B_PAD - B_COLS), F32),
             wl[:, A_COLS + B_COLS:]], axis=1).astype(BF16)
        pvec = jnp.concatenate([
            _tile_lanes(da_qk_norm[l, 0], 8), _tile_lanes(da_qk_norm[l, 1], 8),
            mla_q_norm[l].astype(F32), mla_kv_norm[l].astype(F32),
            _tile_lanes(mla_nope_norm[l, 0], 4), _tile_lanes(mla_nope_norm[l, 1], 4),
            _tile_lanes(mla_rope_norm[l, 0], 4), _tile_lanes(mla_rope_norm[l, 1], 4),
            _tile_lanes(gq_qk_norm[l, 0], 4), _tile_lanes(gq_qk_norm[l, 1], 2)]).reshape(1, P_LEN)
        qa, ka, va, qb, kb, vb, qd, kd, vd, zc = _prep(
            xs, modsel, nw, w_in, rope, seg32, seg64, pvec,
            group_cols(mla_w_uq[l], MLA_NOPE_DIM).astype(BF16), group_cols(mla_w_ukv[l], MLA_NOPE_DIM).astype(BF16),
            n_lat_tiles=n_lat_tiles)

        fa = fast_flag(DA_QK_DIM * gmax2(da_qk_norm[l, 0]), DA_QK_DIM * gmax2(da_qk_norm[l, 1]), DA_QK_DIM ** -0.5)
        fb = fast_flag(MLA_NOPE_DIM * gmax2(mla_nope_norm[l, 0]) + MLA_ROPE_DIM * gmax2(mla_rope_norm[l, 0]),
                       MLA_NOPE_DIM * gmax2(mla_nope_norm[l, 1]) + MLA_ROPE_DIM * gmax2(mla_rope_norm[l, 1]),
                       (MLA_NOPE_DIM + MLA_ROPE_DIM) ** -0.5)
        fd = fast_flag(HEAD_DIM * gmax2(gq_qk_norm[l, 0]), HEAD_DIM * gmax2(gq_qk_norm[l, 1]), HEAD_DIM ** -0.5)
        oa = _attend(fa, qa, ka, va, n_lat=n_lat, n_ctx=n_ctx, need_ctx=not last)
        ob = _attend(fb, qb, kb, vb, n_lat=n_lat, n_ctx=n_ctx, need_ctx=not last)
        od = _attend(fd, qd, kd, vd, n_lat=n_lat, n_ctx=n_ctx, need_ctx=not last)

        lb = lbs[:, l]
        lbp = jnp.zeros((2, 8, HG_FDIM), F32)
        lbp = lbp.at[:, 0].set(jnp.log2(lb)).at[:, 1].set(jnp.log1p(-lb) * LOG2E).at[:, 2].set(1.0 - lb)
        oc = _hgrn(zc, lbp, tri, negm, ones_sub, segh, segg, n_lat=n_lat, n_ctx=n_ctx)

        lam_init = 0.8 - 0.6 * math.exp(-0.3 * l)
        lp = da_lambda[l].astype(F32)
        lam = jnp.exp(jnp.sum(lp[0] * lp[1])) - jnp.exp(jnp.sum(lp[2] * lp[3])) + lam_init
        subln = jnp.concatenate([da_subln[l].astype(F32) * (1.0 - lam_init), jnp.zeros((HEAD_DIM,), F32)])
        opv = jnp.concatenate([jnp.tile(subln, DA_HEADS), _tile_lanes(hg_out_norm[l], HG_HEADS),
                               jnp.full((LANES,), lam, F32)]).reshape(1, O_LEN)
        wo = mix_w_out[l]
        wa = jnp.concatenate([wo[0:GROUP_WIDTH].reshape(DA_HEADS, HEAD_DIM, D_MODEL),
                              jnp.zeros((DA_HEADS, HEAD_DIM, D_MODEL), F32)], axis=1).reshape(2 * GROUP_WIDTH, D_MODEL)
        w_out = jnp.concatenate([wa, wo[GROUP_WIDTH:]], axis=0).astype(BF16)
        xs = _out_proj(xs, modsel, oa, ob, od, oc, zc, w_out, opv, seg64, n_lat_tiles=n_lat_tiles,
                       n_rows=n_lat if last else t)

        xs = _ffn(xs, None, modsel, nw, ffn_w_in[l, 1].astype(BF16), ffn_w_out[l, 1].astype(BF16),
                  sub=2, n_lat_tiles=n_lat_tiles, n_rows=n_lat if last else t)
    return xs
```

```python
import functools
import math

import numpy as np
import jax
import jax.numpy as jnp
from jax import lax
from jax.experimental import pallas as pl
from jax.experimental.pallas import tpu as pltpu

F32 = jnp.float32
BF16 = jnp.bfloat16

D_MODEL = 1024
HEAD_DIM = 64
GROUP_WIDTH = D_MODEL // 4
EPS = 1e-6
ROPE_THETA = 10000.0
GRID_W = 64
HALF = 0.5
N_MOD = 9
D_FF = 2816

DA_HEADS = 4
DA_QK_DIM = 32
MLA_HEADS = 4
MLA_Q_RANK = 256
MLA_KV_RANK = 128
MLA_NOPE_DIM = 64
MLA_ROPE_DIM = 32
HG_HEADS = 4
HG_EXPAND = 128
HG_FDIM = 512
HG_VDIM = 64
GQ_HEADS = 4
GQ_KV_HEADS = 2

A_COLS = 768
B_COLS = 416
C_COLS = 2048
D_COLS = 512
B_PAD = 512
W_COLS = A_COLS + B_PAD + C_COLS + D_COLS

LANES = 128
ROW_TILE = 512
ATT_TQ = 2048
ATT_HEADS = 4
ATT_TK_MAX = 2048
V_ROWS = 128
LOG2E = 1.4426950408889634
FAST_SCORE_BOUND = 64.0
HG_BLOCK = 256
HG_SUB = 16
HG_TGROUP = 8
FF_CHUNK = 512
VMEM_LIMIT = 56 * 1024 * 1024

P_GAQ, P_GAK, P_QN, P_KVN, P_NOPE0, P_NOPE1, P_ROPE0, P_ROPE1, P_GDQ, P_GDK, P_LEN = (
    0, 256, 512, 768, 896, 1152, 1408, 1536, 1664, 1920, 2048)
O_SUBLN, O_HGN, O_LAM, O_LEN = 0, 512, 768, 896


def _cparams(sem):
    return pltpu.CompilerParams(dimension_semantics=sem, vmem_limit_bytes=VMEM_LIMIT)


def _resident(shape):
    nd = len(shape)
    return pl.BlockSpec(shape, lambda *_: (0,) * nd, pipeline_mode=pl.Buffered(1))


def _dot(a, b):
    return jnp.dot(a, b, preferred_element_type=F32)


def _rms_rows(x):
    return x * lax.rsqrt(jnp.mean(x * x, axis=-1, keepdims=True) + EPS)


def _seg_norm(x, seg, size):
    x2 = x * x
    hi = x2.astype(BF16)
    lo = (x2 - hi.astype(F32)).astype(BF16)
    ss = _dot(hi, seg) + _dot(lo, seg)
    return x * lax.rsqrt(ss * (1.0 / size) + EPS)


def _swap_halves(x, half):
    lane = lax.broadcasted_iota(jnp.int32, x.shape, 1)
    up = pltpu.roll(x, LANES - half, 1)
    down = pltpu.roll(x, half, 1)
    return jnp.where((lane % (2 * half)) < half, up, down)


def _rope(x, cos, sin, half):
    parts = []
    for c in range(x.shape[1] // LANES):
        xs = x[:, c * LANES:(c + 1) * LANES]
        parts.append(xs * cos + _swap_halves(xs, half) * sin)
    return parts[0] if len(parts) == 1 else jnp.concatenate(parts, axis=1)


def _silu(x):
    return x / (1.0 + jnp.exp(-x))


def _mod_kernel(c_ref, w_ref, b_ref, o_ref):
    a = _silu(c_ref[...])
    a_hi = a.astype(BF16)
    a_lo = (a - a_hi.astype(F32)).astype(BF16)
    w = w_ref[...]
    w_hi = w.astype(BF16)
    w_lo = (w - w_hi.astype(F32)).astype(BF16)
    o_ref[...] = _dot(a_hi, w_hi) + _dot(a_hi, w_lo) + _dot(a_lo, w_hi) + b_ref[...]


def _modulation(cc, ada_w, ada_b):
    depth = ada_w.shape[0]
    nb = N_MOD * D_MODEL // D_MODEL
    return pl.pallas_call(
        _mod_kernel,
        grid=(depth, nb),
        in_specs=[
            pl.BlockSpec((8, D_MODEL), lambda l, j: (0, 0)),
            pl.BlockSpec((None, D_MODEL, D_MODEL), lambda l, j: (l, 0, j)),
            pl.BlockSpec((None, 1, D_MODEL), lambda l, j: (l, 0, j)),
        ],
        out_specs=pl.BlockSpec((None, 8, D_MODEL), lambda l, j: (l, 0, j)),
        out_shape=jax.ShapeDtypeStruct((depth, 8, N_MOD * D_MODEL), F32),
        compiler_params=_cparams(("parallel", "parallel")),
        name="adaln_mod",
    )(cc, ada_w, ada_b.reshape(depth, 1, N_MOD * D_MODEL))


def _ffn_rows(x, m, nw_ref, win_ref, wout_ref, sub):
    shift, scale, gate = m[3 * sub:3 * sub + 1], m[3 * sub + 1:3 * sub + 2], m[3 * sub + 2:3 * sub + 3]
    h = (_rms_rows(x) * nw_ref[sub:sub + 1]) * (1.0 + scale) + shift
    hb = h.astype(BF16)
    acc = jnp.zeros(x.shape, F32)
    for lo in range(0, D_FF, FF_CHUNK):
        hi = min(lo + FF_CHUNK, D_FF)
        g = _dot(hb, win_ref[:, lo:hi])
        u = _dot(hb, win_ref[:, D_FF + lo:D_FF + hi])
        a = (_silu(g) * u).astype(BF16)
        acc = acc + _dot(a, wout_ref[lo:hi, :])
    return x + HALF * gate * acc


def _ffn_kernel(x_ref, c_ref, mod_ref, nw_ref, win_ref, wout_ref, o_ref, *, sub, n_lat_tiles):
    x = x_ref[...]
    if c_ref is not None:
        x = jnp.where(pl.program_id(1) >= n_lat_tiles, c_ref[...], x)
    o_ref[...] = _ffn_rows(x, mod_ref[...], nw_ref, win_ref, wout_ref, sub)


def _ffn(xs, ctx, modsel, nw, win, wout, *, sub, n_lat_tiles, n_rows):
    bsz = xs.shape[0]
    if ctx is None:
        kern = lambda x_ref, *rest: _ffn_kernel(x_ref, None, *rest, sub=sub, n_lat_tiles=n_lat_tiles)
        x_specs = [pl.BlockSpec((None, ROW_TILE, D_MODEL), lambda b, i: (b, i, 0))]
        x_args = [xs]
    else:
        kern = functools.partial(_ffn_kernel, sub=sub, n_lat_tiles=n_lat_tiles)
        x_specs = [pl.BlockSpec((None, ROW_TILE, D_MODEL), lambda b, i: (b, jnp.minimum(i, n_lat_tiles - 1), 0)),
                   pl.BlockSpec((None, ROW_TILE, D_MODEL), lambda b, i: (b, jnp.maximum(i - n_lat_tiles, 0), 0))]
        x_args = [xs, ctx]
    return pl.pallas_call(
        kern,
        grid=(bsz, pl.cdiv(n_rows, ROW_TILE)),
        in_specs=x_specs + [
            pl.BlockSpec((None, None, N_MOD, D_MODEL),
                         lambda b, i: (b, jnp.where(i >= n_lat_tiles, 1, 0), 0, 0)),
            _resident((3, D_MODEL)),
            _resident((D_MODEL, 2 * D_FF)),
            _resident((D_FF, D_MODEL)),
        ],
        out_specs=pl.BlockSpec((None, ROW_TILE, D_MODEL), lambda b, i: (b, i, 0)),
        out_shape=jax.ShapeDtypeStruct((bsz, n_rows, D_MODEL), F32),
        compiler_params=_cparams(("parallel", "parallel")),
        name=f"ffn_half_{sub}",
    )(*x_args, modsel, nw, win, wout)


def _prep_kernel(x_ref, mod_ref, nw_ref, w_ref, rope_ref, seg32_ref, seg64_ref, pv_ref, wuq_ref, wukv_ref,
                 qa_ref, ka_ref, va_ref, qb_ref, kb_ref, vb_ref, qd_ref, kd_ref, vd_ref, zc_ref):
    x = x_ref[...]
    m = mod_ref[...]
    h = (_rms_rows(x) * nw_ref[1:2]) * (1.0 + m[4:5]) + m[3:4]
    hb = h.astype(BF16)
    seg32 = seg32_ref[...]
    seg64 = seg64_ref[...]
    cos32, sin32 = rope_ref[:, 0:128], rope_ref[:, 128:256]
    cos64, sin64 = rope_ref[:, 256:384], rope_ref[:, 384:512]

    def pv(lo, hi):
        return pv_ref[:, lo:hi]

    za = _dot(hb, w_ref[:, 0:A_COLS])
    qa = _rope(_seg_norm(za[:, 0:256], seg32, 32) * pv(P_GAQ, P_GAQ + 256), cos32, sin32, 16)
    qa = qa * (DA_QK_DIM ** -0.5 * LOG2E)
    ka = _rope(_seg_norm(za[:, 256:512], seg32, 32) * pv(P_GAK, P_GAK + 256), cos32, sin32, 16)
    qa_t = jnp.transpose(qa)
    for hh in range(2 * DA_HEADS):
        qa_ref[hh] = qa_t[32 * hh:32 * hh + 32, :].astype(BF16)
        ka_ref[hh] = ka[:, 32 * hh:32 * hh + 32].astype(BF16)

    def put_v(ref, hh, val_t):
        ref[hh, 0:HEAD_DIM, :] = val_t.astype(BF16)
        if V_ROWS > HEAD_DIM:
            ref[hh, HEAD_DIM:V_ROWS, :] = jnp.zeros((V_ROWS - HEAD_DIM, val_t.shape[1]), BF16)

    va_t = jnp.transpose(za[:, 512:768])
    for hh in range(DA_HEADS):
        put_v(va_ref, hh, va_t[64 * hh:64 * hh + 64, :])

    zb = _dot(hb, w_ref[:, A_COLS:A_COLS + B_PAD])
    cq = _rms_rows(zb[:, 0:256]) * pv(P_QN, P_QN + 256)
    q2 = _dot(cq.astype(BF16), wuq_ref[...])
    ckv = _rms_rows(zb[:, 256:384]) * pv(P_KVN, P_KVN + 128)
    kv2 = _dot(ckv.astype(BF16), wukv_ref[...])
    sb = (MLA_NOPE_DIM + MLA_ROPE_DIM) ** -0.5 * LOG2E
    qn = _seg_norm(q2[:, 0:256], seg64, 64) * pv(P_NOPE0, P_NOPE0 + 256) * sb
    kn = _seg_norm(kv2[:, 0:256], seg64, 64) * pv(P_NOPE1, P_NOPE1 + 256)
    seg32s = seg32[0:128, 0:128]
    qr = _rope(_seg_norm(q2[:, 256:384], seg32s, 32) * pv(P_ROPE0, P_ROPE0 + 128), cos32, sin32, 16) * sb
    kr = _rope(_seg_norm(zb[:, 384:512], seg32s, 32) * pv(P_ROPE1, P_ROPE1 + 128), cos32, sin32, 16)
    zpad = jnp.zeros((x.shape[0], 32), BF16)
    zpad_t = jnp.zeros((32, x.shape[0]), BF16)
    qn_t = jnp.transpose(qn)
    qr_t = jnp.transpose(qr)
    vb_t = jnp.transpose(kv2[:, 256:512])
    for hh in range(MLA_HEADS):
        qb_ref[hh, 0:64, :] = qn_t[64 * hh:64 * hh + 64, :].astype(BF16)
        qb_ref[hh, 64:96, :] = qr_t[32 * hh:32 * hh + 32, :].astype(BF16)
        qb_ref[hh, 96:128, :] = zpad_t
        kb_ref[hh, :, 0:64] = kn[:, 64 * hh:64 * hh + 64].astype(BF16)
        kb_ref[hh, :, 64:96] = kr[:, 0:32].astype(BF16)
        kb_ref[hh, :, 96:128] = zpad
        put_v(vb_ref, hh, vb_t[64 * hh:64 * hh + 64, :])

    zc_ref[...] = _dot(hb, w_ref[:, A_COLS + B_PAD:A_COLS + B_PAD + C_COLS])

    zd = _dot(hb, w_ref[:, A_COLS + B_PAD + C_COLS:W_COLS])
    qd = _rope(_seg_norm(zd[:, 0:256], seg64, 64) * pv(P_GDQ, P_GDQ + 256), cos64, sin64, 32)
    qd = qd * (HEAD_DIM ** -0.5 * LOG2E)
    kd = _rope(_seg_norm(zd[:, 256:384], seg64[0:128, 0:128], 64) * pv(P_GDK, P_GDK + 128), cos64, sin64, 32)
    qd_t = jnp.transpose(qd)
    vd_t = jnp.transpose(zd[:, 384:512])
    for hh in range(GQ_HEADS):
        qd_ref[hh] = qd_t[64 * hh:64 * hh + 64, :].astype(BF16)
    for hh in range(GQ_KV_HEADS):
        kd_ref[hh] = kd[:, 64 * hh:64 * hh + 64].astype(BF16)
        put_v(vd_ref, hh, vd_t[64 * hh:64 * hh + 64, :])


def _prep(xs, modsel, nw, w, rope, seg32, seg64, pvec, wuq, wukv, *, n_lat_tiles):
    bsz, t, _ = xs.shape
    n_tiles = pl.cdiv(t, ROW_TILE)

    def heads(h, d):
        return (pl.BlockSpec((None, h, ROW_TILE, d), lambda b, i: (b, 0, i, 0)),
                jax.ShapeDtypeStruct((bsz, h, t, d), BF16))

    def heads_t(h, d):
        return (pl.BlockSpec((None, h, d, ROW_TILE), lambda b, i: (b, 0, 0, i)),
                jax.ShapeDtypeStruct((bsz, h, d, t), BF16))

    outs = [heads_t(8, 32), heads(8, 32), heads_t(4, V_ROWS),
            heads_t(4, 128), heads(4, 128), heads_t(4, V_ROWS),
            heads_t(4, 64), heads(2, 64), heads_t(2, V_ROWS),
            (pl.BlockSpec((None, ROW_TILE, C_COLS), lambda b, i: (b, i, 0)),
             jax.ShapeDtypeStruct((bsz, t, C_COLS), F32))]
    return pl.pallas_call(
        _prep_kernel,
        grid=(bsz, n_tiles),
        in_specs=[
            pl.BlockSpec((None, ROW_TILE, D_MODEL), lambda b, i: (b, i, 0)),
            pl.BlockSpec((None, None, N_MOD, D_MODEL),
                         lambda b, i: (b, jnp.where(i >= n_lat_tiles, 1, 0), 0, 0)),
            _resident((3, D_MODEL)),
            _resident((D_MODEL, W_COLS)),
            pl.BlockSpec((ROW_TILE, 512), lambda b, i: (i, 0)),
            _resident((256, 256)),
            _resident((256, 256)),
            _resident((1, P_LEN)),
            _resident((MLA_Q_RANK, 384)),
            _resident((MLA_KV_RANK, 512)),
        ],
        out_specs=[o[0] for o in outs],
        out_shape=[o[1] for o in outs],
        compiler_params=_cparams(("parallel", "parallel")),
        name="mix_in_prep",
    )(xs, modsel, nw, w, rope, seg32, seg64, pvec, wuq, wukv)


def _attn_kernel(flag_ref, q_ref, k_ref, v_ref, kc_ref, vc_ref, o_ref, acc_scr, l_scr, m_scr, *, hq, hk, hv, n_kv):
    j = pl.program_id(3)
    fast = flag_ref[0] == 1

    @pl.when(j == 0)
    def _():
        acc_scr[...] = jnp.zeros(acc_scr.shape, F32)
        l_scr[...] = jnp.zeros(l_scr.shape, F32)
        m_scr[...] = jnp.full(m_scr.shape, -jnp.inf, F32)

    def scores(h, kr):
        return _dot(kr[h // (hq // hk)], q_ref[h])

    def step_fast(kr, vr):
        for h in range(hq):
            p = jnp.exp2(scores(h, kr))
            l_scr[h] += jnp.sum(p, axis=0, keepdims=True)
            acc_scr[h] += _dot(vr[h // (hq // hv)], p.astype(BF16))

    def step_safe(kr, vr):
        for h in range(hq):
            s = scores(h, kr)
            m_prev = m_scr[h]
            m_new = jnp.maximum(m_prev, jnp.max(s, axis=0, keepdims=True))
            alpha = jnp.exp2(m_prev - m_new)
            p = jnp.exp2(s - m_new[0:1, :])
            l_scr[h] = alpha * l_scr[h] + jnp.sum(p, axis=0, keepdims=True)
            acc_scr[h] = alpha[0:1, :] * acc_scr[h] + _dot(vr[h // (hq // hv)], p.astype(BF16))
            m_scr[h] = m_new

    def both(kr, vr):
        @pl.when(fast)
        def _():
            step_fast(kr, vr)

        @pl.when(jnp.logical_not(fast))
        def _():
            step_safe(kr, vr)

    @pl.when(j < n_kv)
    def _():
        both(k_ref, v_ref)

    @pl.when(j == n_kv)
    def _():
        both(kc_ref, vc_ref)
        outs = []
        for h in range(hq):
            outs.append(acc_scr[h][0:HEAD_DIM, :] / l_scr[h][0:1, :])
        o_ref[...] = jnp.transpose(jnp.concatenate(outs, axis=0))


def _att_key_tile(hq, n_lat):
    tk = ATT_TK_MAX
    while tk > LANES and (hq * ATT_TQ * tk * 6 > VMEM_LIMIT or n_lat % tk):
        tk //= 2
    return tk


def _attention(flag, q, k, v, prev, *, n_lat, n_ctx, latent):
    bsz, hq_all, dq, t = q.shape
    n_grp = hq_all // ATT_HEADS
    hq = ATT_HEADS
    hk, hv = k.shape[1] // n_grp, v.shape[1] // n_grp
    width = hq * HEAD_DIM
    ctx_blk = n_lat // n_ctx
    tk = _att_key_tile(hq, n_lat)
    if latent:
        tq, nq, n_kv, q_off = ATT_TQ, n_lat // ATT_TQ, n_lat // tk, 0
    else:
        tq, nq, n_kv, q_off = n_ctx, 1, 0, ctx_blk

    def kv_blk(j):
        return jnp.minimum(j, max(n_kv - 1, 0))

    in_specs = [
        pl.BlockSpec((None, hq, dq, tq), lambda b, g, i, j, f: (b, g, 0, i + q_off)),
        pl.BlockSpec((None, hk, tk, dq), lambda b, g, i, j, f: (b, g, kv_blk(j), 0)),
        pl.BlockSpec((None, hv, V_ROWS, tk), lambda b, g, i, j, f: (b, g, 0, kv_blk(j))),
        pl.BlockSpec((None, hk, n_ctx, dq), lambda b, g, i, j, f: (b, g, ctx_blk, 0)),
        pl.BlockSpec((None, hv, V_ROWS, n_ctx), lambda b, g, i, j, f: (b, g, 0, ctx_blk)),
    ]
    args = [flag, q, k, v, k, v]
    aliases = {}
    if not latent:
        in_specs.append(pl.BlockSpec(memory_space=pl.ANY))
        args.append(prev)
        aliases = {6: 0}

    def body(flag_ref, q_ref, k_ref, v_ref, kc_ref, vc_ref, *rest):
        o_ref, acc_scr, l_scr, m_scr = rest[-4:]
        _attn_kernel(flag_ref, q_ref, k_ref, v_ref, kc_ref, vc_ref, o_ref, acc_scr, l_scr, m_scr,
                     hq=hq, hk=hk, hv=hv, n_kv=n_kv)

    grid_spec = pltpu.PrefetchScalarGridSpec(
        num_scalar_prefetch=1,
        grid=(bsz, n_grp, nq, n_kv + 1),
        in_specs=in_specs,
        out_specs=pl.BlockSpec((None, tq, width), lambda b, g, i, j, f: (b, i + q_off, g)),
        scratch_shapes=[pltpu.VMEM((hq, V_ROWS, tq), F32), pltpu.VMEM((hq, 8, tq), F32),
                        pltpu.VMEM((hq, 8, tq), F32)],
    )
    return pl.pallas_call(
        body,
        grid_spec=grid_spec,
        out_shape=jax.ShapeDtypeStruct((bsz, t, hq_all * HEAD_DIM), F32),
        input_output_aliases=aliases,
        compiler_params=_cparams(("parallel", "parallel", "parallel", "arbitrary")),
        name=f"flash_attn_h{hq_all}_d{dq}_{'lat' if latent else 'ctx'}",
    )(*args)


def _attend(flag, q, k, v, *, n_lat, n_ctx, need_ctx):
    o = _attention(flag, q, k, v, None, n_lat=n_lat, n_ctx=n_ctx, latent=True)
    if need_ctx:
        o = _attention(flag, q, k, v, o, n_lat=n_lat, n_ctx=n_ctx, latent=False)
    return o


def _hgrn_block(zq_ref, zx_ref, zi_ref, lb_ref, tri_ref, negm_ref, ones_ref, segh_ref, segg_ref, o_ref, st_scr, oi_scr,
                *, rev):
    ng = HG_BLOCK // HG_SUB
    x2 = zx_ref[...] * LOG2E
    lb = lb_ref[...]
    log_lb, log_1m, one_m = lb[0:1], lb[1:2], lb[2:3]
    e = jnp.exp2(-jnp.abs(x2))
    rcp = 1.0 / (1.0 + e)
    log_sig = jnp.minimum(x2, 0.0) - jnp.log2(1.0 + e)
    b = log_1m + log_sig
    big = jnp.maximum(log_lb, b)
    logf = big + jnp.log2(1.0 + jnp.exp2(-jnp.abs(log_lb - b)))
    kk = one_m * (jnp.where(x2 >= 0.0, e, 1.0) * rcp)
    zq = zq_ref[...]
    q = zq / (1.0 + jnp.exp2(zq * -LOG2E))
    v = zi_ref[:, 0:GROUP_WIDTH]

    h1 = logf.astype(BF16)
    r1 = logf - h1.astype(F32)
    h2 = r1.astype(BF16)
    h3 = (r1 - h2.astype(F32)).astype(BF16)
    tri = tri_ref[...]
    ones = ones_ref[...]
    g = _dot(tri, h1) + _dot(tri, h2) + _dot(tri, h3)
    gtot = _dot(ones, h1) + _dot(ones, h2) + _dot(ones, h3)
    qdec = jnp.exp2(g)
    kdec = jnp.exp2(gtot - g)
    dec = qdec * kdec
    qtb = (q * qdec).astype(BF16)
    ktb = (kk * kdec).astype(BF16)
    vb = v.astype(BF16)

    heads = [(slice(HG_EXPAND * hh, HG_EXPAND * (hh + 1)), slice(HG_VDIM * hh, HG_VDIM * (hh + 1)))
             for hh in range(HG_HEADS)]
    upd = []
    for n in range(ng):
        r = slice(n * HG_SUB, (n + 1) * HG_SUB)
        upd.append(jnp.concatenate(
            [lax.dot_general(vb[r, vs], ktb[r, ks], (((0,), (0,)), ((), ())), preferred_element_type=F32)
             for ks, vs in heads], axis=1))
    st = st_scr[...]
    inter = [None] * ng
    for n in (range(ng - 1, -1, -1) if rev else range(ng)):
        r = slice(n * HG_SUB, (n + 1) * HG_SUB)
        stb = st.astype(BF16)
        inter[n] = jnp.concatenate(
            [lax.dot_general(qtb[r, ks], stb[:, ks], (((1,), (1,)), ((), ())), preferred_element_type=F32)
             for ks, _ in heads], axis=1)
        st = st * dec[n * HG_SUB:n * HG_SUB + 1, :] + upd[n]
    st_scr[...] = st

    g3 = g.reshape(ng, HG_SUB, HG_FDIM)
    k3 = kk.reshape(ng, HG_SUB, HG_FDIM)
    q3 = q.reshape(ng, HG_SUB, HG_FDIM)
    v3 = v.reshape(ng, HG_SUB, GROUP_WIDTH)
    segh = segh_ref[...]
    half = HG_SUB // 2

    def span(t):
        if rev:
            return (half, HG_SUB) if t >= half else (0, HG_SUB)
        return (0, half) if t < half else (0, HG_SUB)

    for t0 in range(0, HG_SUB, HG_TGROUP):
        ts = range(t0, t0 + HG_TGROUP)
        wbs = []
        for t in ts:
            s0, s1 = span(t)
            nm = negm_ref[t, s0:s1, :]
            nm = jnp.concatenate([nm] * (HG_FDIM // LANES), axis=1)[None]
            w = jnp.exp2(g3[:, t:t + 1, :] - g3[:, s0:s1, :] + nm) * k3[:, s0:s1, :] * q3[:, t:t + 1, :]
            wbs.append(w.reshape(ng * (s1 - s0), HG_FDIM).astype(BF16))
        wb = jnp.concatenate(wbs, axis=0)
        hr = wb.shape[0] // 2
        cexp = jnp.concatenate([_dot(wb[0:hr], segh), _dot(wb[hr:], segh)], axis=0)
        row = 0
        for t in ts:
            s0, s1 = span(t)
            nr = ng * (s1 - s0)
            prod = (cexp[row:row + nr] * v3[:, s0:s1, :].reshape(nr, GROUP_WIDTH)).astype(BF16)
            row += nr
            segg = segg_ref[0 if s1 - s0 == half else 1]
            res = _dot(segg[:, 0:nr], prod)
            for c in range(GROUP_WIDTH // LANES):
                oi_scr[c][pl.ds(t, ng, stride=HG_SUB), :] = res[:, c * LANES:(c + 1) * LANES]
    intra = jnp.concatenate([oi_scr[c][...] for c in range(GROUP_WIDTH // LANES)], axis=1)
    o_ref[...] = jnp.concatenate(inter, axis=0) + intra


def _hgrn_kernel(zq_ref, zx_ref, zi_ref, lb_ref, tri_ref, negm_ref, ones_ref, segh_ref, segg_ref, o_ref, st_scr,
                 *oi_scr):
    d = pl.program_id(1)
    j = pl.program_id(2)

    @pl.when(j == 0)
    def _():
        st_scr[...] = jnp.zeros(st_scr.shape, F32)

    args = (zq_ref, zx_ref, zi_ref, lb_ref, tri_ref, negm_ref, ones_ref, segh_ref, segg_ref, o_ref, st_scr, oi_scr)

    @pl.when(d == 0)
    def _():
        _hgrn_block(*args, rev=False)

    @pl.when(d == 1)
    def _():
        _hgrn_block(*args, rev=True)


def _hgrn(zc, lbp, tri, negm, ones, segh, segg, *, n_lat, n_ctx):
    bsz, t, _ = zc.shape
    nl = n_lat // HG_BLOCK
    nc = n_ctx // HG_BLOCK

    def blk(d, j):
        ctx = jnp.where(d == 0, nl + j, nl + nc - 1 - j)
        lat = jnp.where(d == 0, j - nc, nl - 1 - (j - nc))
        return jnp.where(j < nc, ctx, lat)

    return pl.pallas_call(
        _hgrn_kernel,
        grid=(bsz, 2, nl + nc),
        in_specs=[
            pl.BlockSpec((None, HG_BLOCK, HG_FDIM), lambda b, d, j: (b, blk(d, j), 0)),
            pl.BlockSpec((None, HG_BLOCK, HG_FDIM), lambda b, d, j: (b, blk(d, j), 1 + d)),
            pl.BlockSpec((None, HG_BLOCK, HG_FDIM), lambda b, d, j: (b, blk(d, j), 3)),
            pl.BlockSpec((None, 8, HG_FDIM), lambda b, d, j: (d, 0, 0)),
            pl.BlockSpec((None, HG_BLOCK, HG_BLOCK), lambda b, d, j: (d, 0, 0)),
            pl.BlockSpec((None, HG_SUB, HG_SUB, LANES), lambda b, d, j: (d, 0, 0, 0)),
            _resident((HG_BLOCK, HG_BLOCK)),
            _resident((HG_FDIM, GROUP_WIDTH)),
            _resident((2, HG_BLOCK // HG_SUB, HG_BLOCK)),
        ],
        out_specs=pl.BlockSpec((None, None, HG_BLOCK, GROUP_WIDTH), lambda b, d, j: (b, d, blk(d, j), 0)),
        out_shape=jax.ShapeDtypeStruct((bsz, 2, t, GROUP_WIDTH), F32),
        scratch_shapes=[pltpu.VMEM((HG_VDIM, HG_FDIM), F32)]
        + [pltpu.VMEM((HG_BLOCK, LANES), F32) for _ in range(GROUP_WIDTH // LANES)],
        compiler_params=_cparams(("parallel", "parallel", "arbitrary")),
        name="hgrn2_scan",
    )(zc, zc, zc, lbp, tri, negm, ones, segh, segg)


def _out_kernel(x_ref, mod_ref, oa_ref, ob_ref, od_ref, ocf_ref, ocb_ref, g_ref, w_ref, pv_ref, seg64_ref, o_ref):
    x = x_ref[...]
    m = mod_ref[...]
    seg64 = seg64_ref[...]
    lam = pv_ref[:, O_LAM:O_LAM + LANES]

    oa = oa_ref[...]
    lane = lax.broadcasted_iota(jnp.int32, (x.shape[0], LANES), 1)
    ya = []
    for c in range(DA_HEADS):
        pair = oa[:, c * LANES:(c + 1) * LANES]
        diff = pair - lam * pltpu.roll(pair, HEAD_DIM, 1)
        ya.append(jnp.where(lane < HEAD_DIM, diff, 0.0))
    ya = jnp.concatenate(ya, axis=1)
    ya = jnp.concatenate([_seg_norm(ya[:, 0:256], seg64, 64), _seg_norm(ya[:, 256:512], seg64, 64)], axis=1)
    ya = ya * pv_ref[:, O_SUBLN:O_SUBLN + 512]

    oc = ocf_ref[...] + ocb_ref[...]
    yc = _seg_norm(oc, seg64, 64) * pv_ref[:, O_HGN:O_HGN + 256] * _silu(g_ref[...])

    acc = _dot(ya.astype(BF16), w_ref[0:512, :])
    acc = acc + _dot(ob_ref[...].astype(BF16), w_ref[512:768, :])
    acc = acc + _dot(yc.astype(BF16), w_ref[768:1024, :])
    acc = acc + _dot(od_ref[...].astype(BF16), w_ref[1024:1280, :])
    o_ref[...] = x + m[5:6] * acc


def _out_proj(xs, modsel, oa, ob, od, oc, zc, w, pvec, seg64, *, n_lat_tiles, n_rows):
    bsz, t, _ = xs.shape
    n_tiles = pl.cdiv(n_rows, ROW_TILE)
    row = lambda width: pl.BlockSpec((None, ROW_TILE, width), lambda b, i: (b, i, 0))
    return pl.pallas_call(
        _out_kernel,
        grid=(bsz, n_tiles),
        in_specs=[
            row(D_MODEL),
            pl.BlockSpec((None, None, N_MOD, D_MODEL),
                         lambda b, i: (b, jnp.where(i >= n_lat_tiles, 1, 0), 0, 0)),
            row(512), row(256), row(256),
            pl.BlockSpec((None, None, ROW_TILE, GROUP_WIDTH), lambda b, i: (b, 0, i, 0)),
            pl.BlockSpec((None, None, ROW_TILE, GROUP_WIDTH), lambda b, i: (b, 1, i, 0)),
            pl.BlockSpec((None, ROW_TILE, GROUP_WIDTH), lambda b, i: (b, i, C_COLS // GROUP_WIDTH - 1)),
            _resident((1280, D_MODEL)),
            _resident((1, O_LEN)),
            _resident((256, 256)),
        ],
        out_specs=row(D_MODEL),
        out_shape=jax.ShapeDtypeStruct((bsz, t, D_MODEL), F32),
        compiler_params=_cparams(("parallel", "parallel")),
        name="mix_out_proj",
    )(xs, modsel, oa, ob, od, oc, oc, zc, w, pvec, seg64)


def _block_diag_ones(n, size):
    idx = np.arange(n) // size
    return jnp.asarray((idx[:, None] == idx[None, :]).astype(np.float32), dtype=BF16)


def _rope_table(n_lat, n_ctx):
    rows = n_lat // GRID_W
    r = jnp.repeat(jnp.arange(rows), GRID_W).astype(F32)
    col = jnp.tile(jnp.arange(GRID_W), rows).astype(F32)

    def pattern(rot_dim):
        n_freq = rot_dim // 4
        inv = ROPE_THETA ** (-jnp.arange(n_freq, dtype=F32) / n_freq)
        ang = jnp.concatenate([r[:, None] * inv, col[:, None] * inv], axis=-1)
        cos, sin = jnp.cos(ang), jnp.sin(ang)
        cos_l = jnp.tile(jnp.concatenate([cos, cos], axis=-1), (1, LANES // rot_dim))
        sin_l = jnp.tile(jnp.concatenate([-sin, sin], axis=-1), (1, LANES // rot_dim))
        cos_l = jnp.concatenate([cos_l, jnp.ones((n_ctx, LANES), F32)], axis=0)
        sin_l = jnp.concatenate([sin_l, jnp.zeros((n_ctx, LANES), F32)], axis=0)
        return cos_l, sin_l

    c32, s32 = pattern(32)
    c64, s64 = pattern(64)
    return jnp.concatenate([c32, s32, c64, s64], axis=1)


def _tile_lanes(v, reps):
    return jnp.tile(v.astype(F32), reps)


def kernel(x, c, ctx, c_ctx, ada_w, ada_b, norm_w, ffn_w_in, ffn_w_out, mix_w_in, mix_w_out, da_qk_norm, da_lambda,
           da_subln, mla_q_norm, mla_kv_norm, mla_w_uq, mla_w_ukv, mla_nope_norm, mla_rope_norm, hg_lb_logits,
           hg_out_norm, gq_qk_norm):
    bsz, n_lat, _ = x.shape
    n_ctx = ctx.shape[1]
    depth = ada_w.shape[0]
    assert bsz + 1 <= 8 and n_lat % ATT_TQ == 0 and n_lat % n_ctx == 0
    assert n_lat % ROW_TILE == 0 and n_ctx % HG_BLOCK == 0
    t = n_lat + n_ctx
    n_lat_tiles = n_lat // ROW_TILE
    xs = x.astype(F32)

    cc =jnp.zeros((8, D_MODEL), F32).at[0:bsz].set(c).at[bsz].set(c_ctx)
    mods = _modulation(cc, ada_w, ada_b).reshape(depth, 8, N_MOD, D_MODEL)

    p = jax.nn.softmax(hg_lb_logits.astype(F32), axis=1)
    lbs = jnp.maximum(jnp.cumsum(p, axis=1) - p[:, :1], 0.0)

    rope = _rope_table(n_lat, n_ctx)
    seg32 = _block_diag_ones(256, 32)
    seg64 = _block_diag_ones(256, 64)
    ones_sub = _block_diag_ones(HG_BLOCK, HG_SUB)
    ii = np.arange(HG_BLOCK)
    same = (ii[:, None] // HG_SUB) == (ii[None, :] // HG_SUB)
    tri = jnp.asarray(np.stack([same & (ii[None, :] <= ii[:, None]), same & (ii[None, :] >= ii[:, None])])
                      .astype(np.float32), dtype=BF16)
    segh = jnp.asarray((np.arange(HG_FDIM)[:, None] // HG_EXPAND == np.arange(GROUP_WIDTH)[None, :] // HG_VDIM)
                       .astype(np.float32), dtype=BF16)
    n_steps = HG_BLOCK // HG_SUB
    segg_np = np.zeros((2, n_steps, HG_BLOCK), np.float32)
    segg_np[0, :, :HG_BLOCK // 2] = np.arange(n_steps)[:, None] == np.arange(HG_BLOCK // 2)[None, :] // (HG_SUB // 2)
    segg_np[1] = np.arange(n_steps)[:, None] == ii[None, :] // HG_SUB
    segg = jnp.asarray(segg_np, dtype=BF16)
    pp = np.arange(HG_SUB)
    negm_np = np.where(np.stack([pp[None, :] <= pp[:, None], pp[None, :] >= pp[:, None]]), 0.0, -np.inf)
    negm = jnp.asarray(np.broadcast_to(negm_np[..., None], (2, HG_SUB, HG_SUB, LANES)).astype(np.float32))

    def group_cols(w, first):
        r = w.shape[0]
        w3 = w.reshape(r, MLA_HEADS, -1)
        return jnp.concatenate([w3[:, :, :first].reshape(r, -1), w3[:, :, first:].reshape(r, -1)], axis=1)

    def fast_flag(qn2, kn2, scale):
        bound = jnp.sqrt(qn2 * kn2) * (scale * LOG2E * 1.02)
        return (bound <= FAST_SCORE_BOUND).astype(jnp.int32).reshape(1)

    def gmax2(g):
        return jnp.max(jnp.abs(g.astype(F32))) ** 2

    for l in range(depth):
        modsel = jnp.stack([mods[l, 0:bsz], jnp.broadcast_to(mods[l, bsz], (bsz, N_MOD, D_MODEL))], axis=1)
        nw = norm_w[l].astype(F32)
        last = l == depth - 1

        xs = _ffn(xs, ctx.astype(F32) if l == 0 else None, modsel, nw,
                  ffn_w_in[l, 0].astype(BF16), ffn_w_out[l, 0].astype(BF16),
                  sub=0, n_lat_tiles=n_lat_tiles, n_rows=t)

        wl = mix_w_in[l]
        w_in = jnp.concatenate(
            [wl[:, 0:A_COLS], wl[:, A_COLS:A_COLS + B_COLS], jnp.zeros((D_MODEL, B_PAD - B_COLS), F32),
             wl[:, A_COLS + B_COLS:]], axis=1).astype(BF16)
        pvec = jnp.concatenate([
            _tile_lanes(da_qk_norm[l, 0], 8), _tile_lanes(da_qk_norm[l, 1], 8),
            mla_q_norm[l].astype(F32), mla_kv_norm[l].astype(F32),
            _tile_lanes(mla_nope_norm[l, 0], 4), _tile_lanes(mla_nope_norm[l, 1], 4),
            _tile_lanes(mla_rope_norm[l, 0], 4), _tile_lanes(mla_rope_norm[l, 1], 4),
            _tile_lanes(gq_qk_norm[l, 0], 4), _tile_lanes(gq_qk_norm[l, 1], 2)]).reshape(1, P_LEN)
        qa, ka, va, qb, kb, vb, qd, kd, vd, zc = _prep(
            xs, modsel, nw, w_in, rope, seg32, seg64, pvec,
            group_cols(mla_w_uq[l], MLA_NOPE_DIM).astype(BF16), group_cols(mla_w_ukv[l], MLA_NOPE_DIM).astype(BF16),
            n_lat_tiles=n_lat_tiles)

        fa = fast_flag(DA_QK_DIM * gmax2(da_qk_norm[l, 0]), DA_QK_DIM * gmax2(da_qk_norm[l, 1]), DA_QK_DIM ** -0.5)
        fb = fast_flag(MLA_NOPE_DIM * gmax2(mla_nope_norm[l, 0]) + MLA_ROPE_DIM * gmax2(mla_rope_norm[l, 0]),
                       MLA_NOPE_DIM * gmax2(mla_nope_norm[l, 1]) + MLA_ROPE_DIM * gmax2(mla_rope_norm[l, 1]),
                       (MLA_NOPE_DIM + MLA_ROPE_DIM) ** -0.5)
        fd = fast_flag(HEAD_DIM * gmax2(gq_qk_norm[l, 0]), HEAD_DIM * gmax2(gq_qk_norm[l, 1]), HEAD_DIM ** -0.5)
        oa = _attend(fa, qa, ka, va, n_lat=n_lat, n_ctx=n_ctx, need_ctx=not last)
        ob = _attend(fb, qb, kb, vb, n_lat=n_lat, n_ctx=n_ctx, need_ctx=not last)
        od = _attend(fd, qd, kd, vd, n_lat=n_lat, n_ctx=n_ctx, need_ctx=not last)

        lb = lbs[:, l]
        lbp = jnp.zeros((2, 8, HG_FDIM), F32)
        lbp = lbp.at[:, 0].set(jnp.log2(lb)).at[:, 1].set(jnp.log1p(-lb) * LOG2E).at[:, 2].set(1.0 - lb)
        oc = _hgrn(zc, lbp, tri, negm, ones_sub, segh, segg, n_lat=n_lat, n_ctx=n_ctx)

        lam_init = 0.8 - 0.6 * math.exp(-0.3 * l)
        lp = da_lambda[l].astype(F32)
        lam = jnp.exp(jnp.sum(lp[0] * lp[1])) - jnp.exp(jnp.sum(lp[2] * lp[3])) + lam_init
        subln = jnp.concatenate([da_subln[l].astype(F32) * (1.0 - lam_init), jnp.zeros((HEAD_DIM,), F32)])
        opv = jnp.concatenate([jnp.tile(subln, DA_HEADS), _tile_lanes(hg_out_norm[l], HG_HEADS),
                               jnp.full((LANES,), lam, F32)]).reshape(1, O_LEN)
        wo = mix_w_out[l]
        wa = jnp.concatenate([wo[0:GROUP_WIDTH].reshape(DA_HEADS, HEAD_DIM, D_MODEL),
                              jnp.zeros((DA_HEADS, HEAD_DIM, D_MODEL), F32)], axis=1).reshape(2 * GROUP_WIDTH, D_MODEL)
        w_out = jnp.concatenate([wa, wo[GROUP_WIDTH:]], axis=0).astype(BF16)
        xs = _out_proj(xs, modsel, oa, ob, od, oc, zc, w_out, opv, seg64, n_lat_tiles=n_lat_tiles,
                       n_rows=n_lat if last else t)

        xs = _ffn(xs, None, modsel, nw, ffn_w_in[l, 1].astype(BF16), ffn_w_out[l, 1].astype(BF16),
                  sub=2, n_lat_tiles=n_lat_tiles, n_rows=n_lat if last else t)
    return xs
```

```python
import functools
import math

import numpy as np
import jax
import jax.numpy as jnp
from jax import lax
from jax.experimental import pallas as pl
from jax.experimental.pallas import tpu as pltpu

F32 = jnp.float32
BF16 = jnp.bfloat16

D_MODEL = 1024
HEAD_DIM = 64
GROUP_WIDTH = D_MODEL // 4
EPS = 1e-6
ROPE_THETA = 10000.0
GRID_W = 64
HALF = 0.5
N_MOD = 9
D_FF = 2816

DA_HEADS = 4
DA_QK_DIM = 32
MLA_HEADS = 4
MLA_Q_RANK = 256
MLA_KV_RANK = 128
MLA_NOPE_DIM = 64
MLA_ROPE_DIM = 32
HG_HEADS = 4
HG_EXPAND = 128
HG_FDIM = 512
HG_VDIM = 64
GQ_HEADS = 4
GQ_KV_HEADS = 2

A_COLS = 768
B_COLS = 416
C_COLS = 2048
D_COLS = 512
B_PAD = 512
W_COLS = A_COLS + B_PAD + C_COLS + D_COLS

LANES = 128
ROW_TILE = 512
ATT_TQ = 2048
ATT_HEADS = 4
ATT_TK_MAX = 2048
V_ROWS = 128
LOG2E = 1.4426950408889634
FAST_SCORE_BOUND = 64.0
HG_BLOCK = 256
HG_SUB = 16
HG_TGROUP = 8
FF_CHUNK = 512
VMEM_LIMIT = 56 * 1024 * 1024

P_GAQ, P_GAK, P_QN, P_KVN, P_NOPE0, P_NOPE1, P_ROPE0, P_ROPE1, P_GDQ, P_GDK, P_LEN = (
    0, 256, 512, 768, 896, 1152, 1408, 1536, 1664, 1920, 2048)
O_SUBLN, O_HGN, O_LAM, O_LEN = 0, 512, 768, 896


def _cparams(sem):
    return pltpu.CompilerParams(dimension_semantics=sem, vmem_limit_bytes=VMEM_LIMIT)


def _resident(shape):
    nd = len(shape)
    return pl.BlockSpec(shape, lambda *_: (0,) * nd, pipeline_mode=pl.Buffered(1))


def _dot(a, b):
    return jnp.dot(a, b, preferred_element_type=F32)


def _rms_rows(x):
    return x * lax.rsqrt(jnp.mean(x * x, axis=-1, keepdims=True) + EPS)


def _seg_norm(x, seg, size):
    ss = _dot((x * x).astype(BF16), seg)
    return x * lax.rsqrt(ss * (1.0 / size) + EPS)


def _swap_halves(x, half):
    lane = lax.broadcasted_iota(jnp.int32, x.shape, 1)
    up = pltpu.roll(x, LANES - half, 1)
    down = pltpu.roll(x, half, 1)
    return jnp.where((lane % (2 * half)) < half, up, down)


def _rope(x, cos, sin, half):
    parts = []
    for c in range(x.shape[1] // LANES):
        xs = x[:, c * LANES:(c + 1) * LANES]
        parts.append(xs * cos + _swap_halves(xs, half) * sin)
    return parts[0] if len(parts) == 1 else jnp.concatenate(parts, axis=1)


def _silu(x):
    return x / (1.0 + jnp.exp(-x))


def _mod_kernel(c_ref, w_ref, b_ref, o_ref):
    a = _silu(c_ref[...])
    a_hi = a.astype(BF16)
    a_lo = (a - a_hi.astype(F32)).astype(BF16)
    w = w_ref[...]
    w_hi = w.astype(BF16)
    w_lo = (w - w_hi.astype(F32)).astype(BF16)
    o_ref[...] = _dot(a_hi, w_hi) + _dot(a_hi, w_lo) + _dot(a_lo, w_hi) + b_ref[...]


def _modulation(cc, ada_w, ada_b):
    depth = ada_w.shape[0]
    nb = N_MOD * D_MODEL // D_MODEL
    return pl.pallas_call(
        _mod_kernel,
        grid=(depth, nb),
        in_specs=[
            pl.BlockSpec((8, D_MODEL), lambda l, j: (0, 0)),
            pl.BlockSpec((None, D_MODEL, D_MODEL), lambda l, j: (l, 0, j)),
            pl.BlockSpec((None, 1, D_MODEL), lambda l, j: (l, 0, j)),
        ],
        out_specs=pl.BlockSpec((None, 8, D_MODEL), lambda l, j: (l, 0, j)),
        out_shape=jax.ShapeDtypeStruct((depth, 8, N_MOD * D_MODEL), F32),
        compiler_params=_cparams(("parallel", "parallel")),
        name="adaln_mod",
    )(cc, ada_w, ada_b.reshape(depth, 1, N_MOD * D_MODEL))


def _ffn_rows(x, m, nw_ref, win_ref, wout_ref, sub):
    shift, scale, gate = m[3 * sub:3 * sub + 1], m[3 * sub + 1:3 * sub + 2], m[3 * sub + 2:3 * sub + 3]
    h = (_rms_rows(x) * nw_ref[sub:sub + 1]) * (1.0 + scale) + shift
    hb = h.astype(BF16)
    acc = jnp.zeros(x.shape, F32)
    for lo in range(0, D_FF, FF_CHUNK):
        hi = min(lo + FF_CHUNK, D_FF)
        g = _dot(hb, win_ref[:, lo:hi])
        u = _dot(hb, win_ref[:, D_FF + lo:D_FF + hi])
        a = (_silu(g) * u).astype(BF16)
        acc = acc + _dot(a, wout_ref[lo:hi, :])
    return x + HALF * gate * acc


def _ffn_kernel(x_ref, c_ref, mod_ref, nw_ref, win_ref, wout_ref, o_ref, *, sub, n_lat_tiles):
    x = x_ref[...]
    if c_ref is not None:
        x = jnp.where(pl.program_id(1) >= n_lat_tiles, c_ref[...], x)
    o_ref[...] = _ffn_rows(x, mod_ref[...], nw_ref, win_ref, wout_ref, sub)


def _ffn(xs, ctx, modsel, nw, win, wout, *, sub, n_lat_tiles, n_rows):
    bsz = xs.shape[0]
    if ctx is None:
        kern = lambda x_ref, *rest: _ffn_kernel(x_ref, None, *rest, sub=sub, n_lat_tiles=n_lat_tiles)
        x_specs = [pl.BlockSpec((None, ROW_TILE, D_MODEL), lambda b, i: (b, i, 0))]
        x_args = [xs]
    else:
        kern = functools.partial(_ffn_kernel, sub=sub, n_lat_tiles=n_lat_tiles)
        x_specs = [pl.BlockSpec((None, ROW_TILE, D_MODEL), lambda b, i: (b, jnp.minimum(i, n_lat_tiles - 1), 0)),
                   pl.BlockSpec((None, ROW_TILE, D_MODEL), lambda b, i: (b, jnp.maximum(i - n_lat_tiles, 0), 0))]
        x_args = [xs, ctx]
    return pl.pallas_call(
        kern,
        grid=(bsz, pl.cdiv(n_rows, ROW_TILE)),
        in_specs=x_specs + [
            pl.BlockSpec((None, None, N_MOD, D_MODEL),
                         lambda b, i: (b, jnp.where(i >= n_lat_tiles, 1, 0), 0, 0)),
            _resident((3, D_MODEL)),
            _resident((D_MODEL, 2 * D_FF)),
            _resident((D_FF, D_MODEL)),
        ],
        out_specs=pl.BlockSpec((None, ROW_TILE, D_MODEL), lambda b, i: (b, i, 0)),
        out_shape=jax.ShapeDtypeStruct((bsz, n_rows, D_MODEL), F32),
        compiler_params=_cparams(("parallel", "parallel")),
        name=f"ffn_half_{sub}",
    )(*x_args, modsel, nw, win, wout)


def _prep_kernel(x_ref, mod_ref, nw_ref, w_ref, rope_ref, seg32_ref, seg64_ref, pv_ref, wuq_ref, wukv_ref,
                 qa_ref, ka_ref, va_ref, qb_ref, kb_ref, vb_ref, qd_ref, kd_ref, vd_ref, zc_ref):
    x = x_ref[...]
    m = mod_ref[...]
    h = (_rms_rows(x) * nw_ref[1:2]) * (1.0 + m[4:5]) + m[3:4]
    hb = h.astype(BF16)
    seg32 = seg32_ref[...]
    seg64 = seg64_ref[...]
    cos32, sin32 = rope_ref[:, 0:128], rope_ref[:, 128:256]
    cos64, sin64 = rope_ref[:, 256:384], rope_ref[:, 384:512]

    def pv(lo, hi):
        return pv_ref[:, lo:hi]

    za = _dot(hb, w_ref[:, 0:A_COLS])
    qa = _rope(_seg_norm(za[:, 0:256], seg32, 32) * pv(P_GAQ, P_GAQ + 256), cos32, sin32, 16)
    qa = qa * (DA_QK_DIM ** -0.5 * LOG2E)
    ka = _rope(_seg_norm(za[:, 256:512], seg32, 32) * pv(P_GAK, P_GAK + 256), cos32, sin32, 16)
    qa_t = jnp.transpose(qa)
    for hh in range(2 * DA_HEADS):
        qa_ref[hh] = qa_t[32 * hh:32 * hh + 32, :].astype(BF16)
        ka_ref[hh] = ka[:, 32 * hh:32 * hh + 32].astype(BF16)

    def put_v(ref, hh, val_t):
        ref[hh, 0:HEAD_DIM, :] = val_t.astype(BF16)
        if V_ROWS > HEAD_DIM:
            ref[hh, HEAD_DIM:V_ROWS, :] = jnp.zeros((V_ROWS - HEAD_DIM, val_t.shape[1]), BF16)

    va_t = jnp.transpose(za[:, 512:768])
    for hh in range(DA_HEADS):
        put_v(va_ref, hh, va_t[64 * hh:64 * hh + 64, :])

    zb = _dot(hb, w_ref[:, A_COLS:A_COLS + B_PAD])
    cq = _rms_rows(zb[:, 0:256]) * pv(P_QN, P_QN + 256)
    q2 = _dot(cq.astype(BF16), wuq_ref[...])
    ckv = _rms_rows(zb[:, 256:384]) * pv(P_KVN, P_KVN + 128)
    kv2 = _dot(ckv.astype(BF16), wukv_ref[...])
    sb = (MLA_NOPE_DIM + MLA_ROPE_DIM) ** -0.5 * LOG2E
    qn = _seg_norm(q2[:, 0:256], seg64, 64) * pv(P_NOPE0, P_NOPE0 + 256) * sb
    kn = _seg_norm(kv2[:, 0:256], seg64, 64) * pv(P_NOPE1, P_NOPE1 + 256)
    seg32s = seg32[0:128, 0:128]
    qr = _rope(_seg_norm(q2[:, 256:384], seg32s, 32) * pv(P_ROPE0, P_ROPE0 + 128), cos32, sin32, 16) * sb
    kr = _rope(_seg_norm(zb[:, 384:512], seg32s, 32) * pv(P_ROPE1, P_ROPE1 + 128), cos32, sin32, 16)
    zpad = jnp.zeros((x.shape[0], 32), BF16)
    zpad_t = jnp.zeros((32, x.shape[0]), BF16)
    qn_t = jnp.transpose(qn)
    qr_t = jnp.transpose(qr)
    vb_t = jnp.transpose(kv2[:, 256:512])
    for hh in range(MLA_HEADS):
        qb_ref[hh, 0:64, :] = qn_t[64 * hh:64 * hh + 64, :].astype(BF16)
        qb_ref[hh, 64:96, :] = qr_t[32 * hh:32 * hh + 32, :].astype(BF16)
        qb_ref[hh, 96:128, :] = zpad_t
        kb_ref[hh, :, 0:64] = kn[:, 64 * hh:64 * hh + 64].astype(BF16)
        kb_ref[hh, :, 64:96] = kr[:, 0:32].astype(BF16)
        kb_ref[hh, :, 96:128] = zpad
        put_v(vb_ref, hh, vb_t[64 * hh:64 * hh + 64, :])

    zc_ref[...] = _dot(hb, w_ref[:, A_COLS + B_PAD:A_COLS + B_PAD + C_COLS])

    zd = _dot(hb, w_ref[:, A_COLS + B_PAD + C_COLS:W_COLS])
    qd = _rope(_seg_norm(zd[:, 0:256], seg64, 64) * pv(P_GDQ, P_GDQ + 256), cos64, sin64, 32)
    qd = qd * (HEAD_DIM ** -0.5 * LOG2E)
    kd = _rope(_seg_norm(zd[:, 256:384], seg64[0:128, 0:128], 64) * pv(P_GDK, P_GDK + 128), cos64, sin64, 32)
    qd_t = jnp.transpose(qd)
    vd_t = jnp.transpose(zd[:, 384:512])
    for hh in range(GQ_HEADS):
        qd_ref[hh] = qd_t[64 * hh:64 * hh + 64, :].astype(BF16)
    for hh in range(GQ_KV_HEADS):
        kd_ref[hh] = kd[:, 64 * hh:64 * hh + 64].astype(BF16)
        put_v(vd_ref, hh, vd_t[64 * hh:64 * hh + 64, :])


def _prep(xs, modsel, nw, w, rope, seg32, seg64, pvec, wuq, wukv, *, n_lat_tiles):
    bsz, t, _ = xs.shape
    n_tiles = pl.cdiv(t, ROW_TILE)

    def heads(h, d):
        return (pl.BlockSpec((None, h, ROW_TILE, d), lambda b, i: (b, 0, i, 0)),
                jax.ShapeDtypeStruct((bsz, h, t, d), BF16))

    def heads_t(h, d):
        return (pl.BlockSpec((None, h, d, ROW_TILE), lambda b, i: (b, 0, 0, i)),
                jax.ShapeDtypeStruct((bsz, h, d, t), BF16))

    outs = [heads_t(8, 32), heads(8, 32), heads_t(4, V_ROWS),
            heads_t(4, 128), heads(4, 128), heads_t(4, V_ROWS),
            heads_t(4, 64), heads(2, 64), heads_t(2, V_ROWS),
            (pl.BlockSpec((None, ROW_TILE, C_COLS), lambda b, i: (b, i, 0)),
             jax.ShapeDtypeStruct((bsz, t, C_COLS), F32))]
    return pl.pallas_call(
        _prep_kernel,
        grid=(bsz, n_tiles),
        in_specs=[
            pl.BlockSpec((None, ROW_TILE, D_MODEL), lambda b, i: (b, i, 0)),
            pl.BlockSpec((None, None, N_MOD, D_MODEL),
                         lambda b, i: (b, jnp.where(i >= n_lat_tiles, 1, 0), 0, 0)),
            _resident((3, D_MODEL)),
            _resident((D_MODEL, W_COLS)),
            pl.BlockSpec((ROW_TILE, 512), lambda b, i: (i, 0)),
            _resident((256, 256)),
            _resident((256, 256)),
            _resident((1, P_LEN)),
            _resident((MLA_Q_RANK, 384)),
            _resident((MLA_KV_RANK, 512)),
        ],
        out_specs=[o[0] for o in outs],
        out_shape=[o[1] for o in outs],
        compiler_params=_cparams(("parallel", "parallel")),
        name="mix_in_prep",
    )(xs, modsel, nw, w, rope, seg32, seg64, pvec, wuq, wukv)


def _attn_kernel(flag_ref, q_ref, k_ref, v_ref, kc_ref, vc_ref, o_ref, acc_scr, l_scr, m_scr, *, hq, hk, hv, n_kv):
    j = pl.program_id(3)
    fast = flag_ref[0] == 1

    @pl.when(j == 0)
    def _():
        acc_scr[...] = jnp.zeros(acc_scr.shape, F32)
        l_scr[...] = jnp.zeros(l_scr.shape, F32)
        m_scr[...] = jnp.full(m_scr.shape, -jnp.inf, F32)

    def scores(h, kr):
        return _dot(kr[h // (hq // hk)], q_ref[h])

    def step_fast(kr, vr):
        for h in range(hq):
            p = jnp.exp2(scores(h, kr))
            l_scr[h] += jnp.sum(p, axis=0, keepdims=True)
            acc_scr[h] += _dot(vr[h // (hq // hv)], p.astype(BF16))

    def step_safe(kr, vr):
        for h in range(hq):
            s = scores(h, kr)
            m_prev = m_scr[h]
            m_new = jnp.maximum(m_prev, jnp.max(s, axis=0, keepdims=True))
            alpha = jnp.exp2(m_prev - m_new)
            p = jnp.exp2(s - m_new[0:1, :])
            l_scr[h] = alpha * l_scr[h] + jnp.sum(p, axis=0, keepdims=True)
            acc_scr[h] = alpha[0:1, :] * acc_scr[h] + _dot(vr[h // (hq // hv)], p.astype(BF16))
            m_scr[h] = m_new

    def both(kr, vr):
        @pl.when(fast)
        def _():
            step_fast(kr, vr)

        @pl.when(jnp.logical_not(fast))
        def _():
            step_safe(kr, vr)

    @pl.when(j < n_kv)
    def _():
        both(k_ref, v_ref)

    @pl.when(j == n_kv)
    def _():
        both(kc_ref, vc_ref)
        outs = []
        for h in range(hq):
            outs.append(acc_scr[h][0:HEAD_DIM, :] / l_scr[h][0:1, :])
        o_ref[...] = jnp.transpose(jnp.concatenate(outs, axis=0))


def _att_key_tile(hq, n_lat):
    tk = ATT_TK_MAX
    while tk > LANES and (hq * ATT_TQ * tk * 6 > VMEM_LIMIT or n_lat % tk):
        tk //= 2
    return tk


def _attention(flag, q, k, v, prev, *, n_lat, n_ctx, latent):
    bsz, hq_all, dq, t = q.shape
    n_grp = hq_all // ATT_HEADS
    hq = ATT_HEADS
    hk, hv = k.shape[1] // n_grp, v.shape[1] // n_grp
    width = hq * HEAD_DIM
    ctx_blk = n_lat // n_ctx
    tk = _att_key_tile(hq, n_lat)
    if latent:
        tq, nq, n_kv, q_off = ATT_TQ, n_lat // ATT_TQ, n_lat // tk, 0
    else:
        tq, nq, n_kv, q_off = n_ctx, 1, 0, ctx_blk

    def kv_blk(j):
        return jnp.minimum(j, max(n_kv - 1, 0))

    in_specs = [
        pl.BlockSpec((None, hq, dq, tq), lambda b, g, i, j, f: (b, g, 0, i + q_off)),
        pl.BlockSpec((None, hk, tk, dq), lambda b, g, i, j, f: (b, g, kv_blk(j), 0)),
        pl.BlockSpec((None, hv, V_ROWS, tk), lambda b, g, i, j, f: (b, g, 0, kv_blk(j))),
        pl.BlockSpec((None, hk, n_ctx, dq), lambda b, g, i, j, f: (b, g, ctx_blk, 0)),
        pl.BlockSpec((None, hv, V_ROWS, n_ctx), lambda b, g, i, j, f: (b, g, 0, ctx_blk)),
    ]
    args = [flag, q, k, v, k, v]
    aliases = {}
    if not latent:
        in_specs.append(pl.BlockSpec(memory_space=pl.ANY))
        args.append(prev)
        aliases = {6: 0}

    def body(flag_ref, q_ref, k_ref, v_ref, kc_ref, vc_ref, *rest):
        o_ref, acc_scr, l_scr, m_scr = rest[-4:]
        _attn_kernel(flag_ref, q_ref, k_ref, v_ref, kc_ref, vc_ref, o_ref, acc_scr, l_scr, m_scr,
                     hq=hq, hk=hk, hv=hv, n_kv=n_kv)

    grid_spec = pltpu.PrefetchScalarGridSpec(
        num_scalar_prefetch=1,
        grid=(bsz, n_grp, nq, n_kv + 1),
        in_specs=in_specs,
        out_specs=pl.BlockSpec((None, tq, width), lambda b, g, i, j, f: (b, i + q_off, g)),
        scratch_shapes=[pltpu.VMEM((hq, V_ROWS, tq), F32), pltpu.VMEM((hq, 8, tq), F32),
                        pltpu.VMEM((hq, 8, tq), F32)],
    )
    return pl.pallas_call(
        body,
        grid_spec=grid_spec,
        out_shape=jax.ShapeDtypeStruct((bsz, t, hq_all * HEAD_DIM), F32),
        input_output_aliases=aliases,
        compiler_params=_cparams(("parallel", "parallel", "parallel", "arbitrary")),
        name=f"flash_attn_h{hq_all}_d{dq}_{'lat' if latent else 'ctx'}",
    )(*args)


def _attend(flag, q, k, v, *, n_lat, n_ctx, need_ctx):
    o = _attention(flag, q, k, v, None, n_lat=n_lat, n_ctx=n_ctx, latent=True)
    if need_ctx:
        o = _attention(flag, q, k, v, o, n_lat=n_lat, n_ctx=n_ctx, latent=False)
    return o


def _hgrn_block(zq_ref, zx_ref, zi_ref, lb_ref, tri_ref, negm_ref, ones_ref, segh_ref, segg_ref, o_ref, st_scr, oi_scr,
                *, rev):
    ng = HG_BLOCK // HG_SUB
    x2 = zx_ref[...] * LOG2E
    lb = lb_ref[...]
    log_lb, log_1m, one_m = lb[0:1], lb[1:2], lb[2:3]
    e = jnp.exp2(-jnp.abs(x2))
    rcp = 1.0 / (1.0 + e)
    log_sig = jnp.minimum(x2, 0.0) - jnp.log2(1.0 + e)
    b = log_1m + log_sig
    big = jnp.maximum(log_lb, b)
    logf = big + jnp.log2(1.0 + jnp.exp2(-jnp.abs(log_lb - b)))
    kk = one_m * (jnp.where(x2 >= 0.0, e, 1.0) * rcp)
    zq = zq_ref[...]
    q = zq / (1.0 + jnp.exp2(zq * -LOG2E))
    v = zi_ref[:, 0:GROUP_WIDTH]

    h1 = logf.astype(BF16)
    r1 = logf - h1.astype(F32)
    h2 = r1.astype(BF16)
    h3 = (r1 - h2.astype(F32)).astype(BF16)
    tri = tri_ref[...]
    ones = ones_ref[...]
    g = _dot(tri, h1) + _dot(tri, h2) + _dot(tri, h3)
    gtot = _dot(ones, h1) + _dot(ones, h2) + _dot(ones, h3)
    qdec = jnp.exp2(g)
    kdec = jnp.exp2(gtot - g)
    dec = qdec * kdec
    qtb = (q * qdec).astype(BF16)
    ktb = (kk * kdec).astype(BF16)
    vb = v.astype(BF16)

    heads = [(slice(HG_EXPAND * hh, HG_EXPAND * (hh + 1)), slice(HG_VDIM * hh, HG_VDIM * (hh + 1)))
             for hh in range(HG_HEADS)]
    upd = []
    for n in range(ng):
        r = slice(n * HG_SUB, (n + 1) * HG_SUB)
        upd.append(jnp.concatenate(
            [lax.dot_general(vb[r, vs], ktb[r, ks], (((0,), (0,)), ((), ())), preferred_element_type=F32)
             for ks, vs in heads], axis=1))
    st = st_scr[...]
    inter = [None] * ng
    for n in (range(ng - 1, -1, -1) if rev else range(ng)):
        r = slice(n * HG_SUB, (n + 1) * HG_SUB)
        stb = st.astype(BF16)
        inter[n] = jnp.concatenate(
            [lax.dot_general(qtb[r, ks], stb[:, ks], (((1,), (1,)), ((), ())), preferred_element_type=F32)
             for ks, _ in heads], axis=1)
        st = st * dec[n * HG_SUB:n * HG_SUB + 1, :] + upd[n]
    st_scr[...] = st

    g3 = g.reshape(ng, HG_SUB, HG_FDIM)
    k3 = kk.reshape(ng, HG_SUB, HG_FDIM)
    q3 = q.reshape(ng, HG_SUB, HG_FDIM)
    v3 = v.reshape(ng, HG_SUB, GROUP_WIDTH)
    segh = segh_ref[...]
    half = HG_SUB // 2

    def span(t):
        if rev:
            return (half, HG_SUB) if t >= half else (0, HG_SUB)
        return (0, half) if t < half else (0, HG_SUB)

    for t0 in range(0, HG_SUB, HG_TGROUP):
        ts = range(t0, t0 + HG_TGROUP)
        wbs = []
        for t in ts:
            s0, s1 = span(t)
            tt = (t // half) * half
            nm = jnp.concatenate([negm_ref[t, tt:tt + half, :]] * (HG_FDIM // LANES), axis=1)[None]
            gt = g3[:, t:t + 1, :]
            parts = [gt - g3[:, a:a + half, :] for a in range(s0, s1, half)]
            parts = [p + nm if a == tt else p for a, p in zip(range(s0, s1, half), parts)]
            diff = parts[0] if len(parts) == 1 else jnp.concatenate(parts, axis=1)
            w = jnp.exp2(diff) * k3[:, s0:s1, :] * q3[:, t:t + 1, :]
            wbs.append(w.reshape(ng * (s1 - s0), HG_FDIM).astype(BF16))
        wb = jnp.concatenate(wbs, axis=0)
        hr = wb.shape[0] // 2
        cexp = jnp.concatenate([_dot(wb[0:hr], segh), _dot(wb[hr:], segh)], axis=0)
        row = 0
        for t in ts:
            s0, s1 = span(t)
            nr = ng * (s1 - s0)
            prod = (cexp[row:row + nr] * v3[:, s0:s1, :].reshape(nr, GROUP_WIDTH)).astype(BF16)
            row += nr
            segg = segg_ref[0 if s1 - s0 == half else 1]
            res = _dot(segg[:, 0:nr], prod)
            for c in range(GROUP_WIDTH // LANES):
                oi_scr[c][pl.ds(t, ng, stride=HG_SUB), :] = res[:, c * LANES:(c + 1) * LANES]
    intra = jnp.concatenate([oi_scr[c][...] for c in range(GROUP_WIDTH // LANES)], axis=1)
    o_ref[...] = jnp.concatenate(inter, axis=0) + intra


def _hgrn_kernel(zq_ref, zx_ref, zi_ref, lb_ref, tri_ref, negm_ref, ones_ref, segh_ref, segg_ref, o_ref, st_scr,
                 *oi_scr):
    d = pl.program_id(1)
    j = pl.program_id(2)

    @pl.when(j == 0)
    def _():
        st_scr[...] = jnp.zeros(st_scr.shape, F32)

    args = (zq_ref, zx_ref, zi_ref, lb_ref, tri_ref, negm_ref, ones_ref, segh_ref, segg_ref, o_ref, st_scr, oi_scr)

    @pl.when(d == 0)
    def _():
        _hgrn_block(*args, rev=False)

    @pl.when(d == 1)
    def _():
        _hgrn_block(*args, rev=True)


def _hgrn(zc, lbp, tri, negm, ones, segh, segg, *, n_lat, n_ctx):
    bsz, t, _ = zc.shape
    nl = n_lat // HG_BLOCK
    nc = n_ctx // HG_BLOCK

    def blk(d, j):
        ctx = jnp.where(d == 0, nl + j, nl + nc - 1 - j)
        lat = jnp.where(d == 0, j - nc, nl - 1 - (j - nc))
        return jnp.where(j < nc, ctx, lat)

    return pl.pallas_call(
        _hgrn_kernel,
        grid=(bsz, 2, nl + nc),
        in_specs=[
            pl.BlockSpec((None, HG_BLOCK, HG_FDIM), lambda b, d, j: (b, blk(d, j), 0)),
            pl.BlockSpec((None, HG_BLOCK, HG_FDIM), lambda b, d, j: (b, blk(d, j), 1 + d)),
            pl.BlockSpec((None, HG_BLOCK, HG_FDIM), lambda b, d, j: (b, blk(d, j), 3)),
            pl.BlockSpec((None, 8, HG_FDIM), lambda b, d, j: (d, 0, 0)),
            pl.BlockSpec((None, HG_BLOCK, HG_BLOCK), lambda b, d, j: (d, 0, 0)),
            pl.BlockSpec((None, HG_SUB, HG_SUB, LANES), lambda b, d, j: (d, 0, 0, 0)),
            _resident((HG_BLOCK, HG_BLOCK)),
            _resident((HG_FDIM, GROUP_WIDTH)),
            _resident((2, HG_BLOCK // HG_SUB, HG_BLOCK)),
        ],
        out_specs=pl.BlockSpec((None, None, HG_BLOCK, GROUP_WIDTH), lambda b, d, j: (b, d, blk(d, j), 0)),
        out_shape=jax.ShapeDtypeStruct((bsz, 2, t, GROUP_WIDTH), F32),
        scratch_shapes=[pltpu.VMEM((HG_VDIM, HG_FDIM), F32)]
        + [pltpu.VMEM((HG_BLOCK, LANES), F32) for _ in range(GROUP_WIDTH // LANES)],
        compiler_params=_cparams(("parallel", "parallel", "arbitrary")),
        name="hgrn2_scan",
    )(zc, zc, zc, lbp, tri, negm, ones, segh, segg)


def _out_kernel(x_ref, mod_ref, oa_ref, ob_ref, od_ref, ocf_ref, ocb_ref, g_ref, w_ref, pv_ref, seg64_ref, o_ref):
    x = x_ref[...]
    m = mod_ref[...]
    seg64 = seg64_ref[...]
    lam = pv_ref[:, O_LAM:O_LAM + LANES]

    oa = oa_ref[...]
    lane = lax.broadcasted_iota(jnp.int32, (x.shape[0], LANES), 1)
    ya = []
    for c in range(DA_HEADS):
        pair = oa[:, c * LANES:(c + 1) * LANES]
        diff = pair - lam * pltpu.roll(pair, HEAD_DIM, 1)
        ya.append(jnp.where(lane < HEAD_DIM, diff, 0.0))
    ya = jnp.concatenate(ya, axis=1)
    ya = jnp.concatenate([_seg_norm(ya[:, 0:256], seg64, 64), _seg_norm(ya[:, 256:512], seg64, 64)], axis=1)
    ya = ya * pv_ref[:, O_SUBLN:O_SUBLN + 512]

    oc = ocf_ref[...] + ocb_ref[...]
    yc = _seg_norm(oc, seg64, 64) * pv_ref[:, O_HGN:O_HGN + 256] * _silu(g_ref[...])

    acc = _dot(ya.astype(BF16), w_ref[0:512, :])
    acc = acc + _dot(ob_ref[...].astype(BF16), w_ref[512:768, :])
    acc = acc + _dot(yc.astype(BF16), w_ref[768:1024, :])
    acc = acc + _dot(od_ref[...].astype(BF16), w_ref[1024:1280, :])
    o_ref[...] = x + m[5:6] * acc


def _out_proj(xs, modsel, oa, ob, od, oc, zc, w, pvec, seg64, *, n_lat_tiles, n_rows):
    bsz, t, _ = xs.shape
    n_tiles = pl.cdiv(n_rows, ROW_TILE)
    row = lambda width: pl.BlockSpec((None, ROW_TILE, width), lambda b, i: (b, i, 0))
    return pl.pallas_call(
        _out_kernel,
        grid=(bsz, n_tiles),
        in_specs=[
            row(D_MODEL),
            pl.BlockSpec((None, None, N_MOD, D_MODEL),
                         lambda b, i: (b, jnp.where(i >= n_lat_tiles, 1, 0), 0, 0)),
            row(512), row(256), row(256),
            pl.BlockSpec((None, None, ROW_TILE, GROUP_WIDTH), lambda b, i: (b, 0, i, 0)),
            pl.BlockSpec((None, None, ROW_TILE, GROUP_WIDTH), lambda b, i: (b, 1, i, 0)),
            pl.BlockSpec((None, ROW_TILE, GROUP_WIDTH), lambda b, i: (b, i, C_COLS // GROUP_WIDTH - 1)),
            _resident((1280, D_MODEL)),
            _resident((1, O_LEN)),
            _resident((256, 256)),
        ],
        out_specs=row(D_MODEL),
        out_shape=jax.ShapeDtypeStruct((bsz, t, D_MODEL), F32),
        compiler_params=_cparams(("parallel", "parallel")),
        name="mix_out_proj",
    )(xs, modsel, oa, ob, od, oc, oc, zc, w, pvec, seg64)


def _block_diag_ones(n, size):
    idx = np.arange(n) // size
    return jnp.asarray((idx[:, None] == idx[None, :]).astype(np.float32), dtype=BF16)


def _rope_table(n_lat, n_ctx):
    rows = n_lat // GRID_W
    r = jnp.repeat(jnp.arange(rows), GRID_W).astype(F32)
    col = jnp.tile(jnp.arange(GRID_W), rows).astype(F32)

    def pattern(rot_dim):
        n_freq = rot_dim // 4
        inv = ROPE_THETA ** (-jnp.arange(n_freq, dtype=F32) / n_freq)
        ang = jnp.concatenate([r[:, None] * inv, col[:, None] * inv], axis=-1)
        cos, sin = jnp.cos(ang), jnp.sin(ang)
        cos_l = jnp.tile(jnp.concatenate([cos, cos], axis=-1), (1, LANES // rot_dim))
        sin_l = jnp.tile(jnp.concatenate([-sin, sin], axis=-1), (1, LANES // rot_dim))
        cos_l = jnp.concatenate([cos_l, jnp.ones((n_ctx, LANES), F32)], axis=0)
        sin_l = jnp.concatenate([sin_l, jnp.zeros((n_ctx, LANES), F32)], axis=0)
        return cos_l, sin_l

    c32, s32 = pattern(32)
    c64, s64 = pattern(64)
    return jnp.concatenate([c32, s32, c64, s64], axis=1)


def _tile_lanes(v, reps):
    return jnp.tile(v.astype(F32), reps)


def kernel(x, c, ctx, c_ctx, ada_w, ada_b, norm_w, ffn_w_in, ffn_w_out, mix_w_in, mix_w_out, da_qk_norm, da_lambda,
           da_subln, mla_q_norm, mla_kv_norm, mla_w_uq, mla_w_ukv, mla_nope_norm, mla_rope_norm, hg_lb_logits,
           hg_out_norm, gq_qk_norm):
    bsz, n_lat, _ = x.shape
    n_ctx = ctx.shape[1]
    depth = ada_w.shape[0]
    assert bsz + 1 <= 8 and n_lat % ATT_TQ == 0 and n_lat % n_ctx == 0
    assert n_lat % ROW_TILE == 0 and n_ctx % HG_BLOCK == 0
    t = n_lat + n_ctx
    n_lat_tiles = n_lat // ROW_TILE
    xs = x.astype(F32)

    cc =jnp.zeros((8, D_MODEL), F32).at[0:bsz].set(c).at[bsz].set(c_ctx)
    mods = _modulation(cc, ada_w, ada_b).reshape(depth, 8, N_MOD, D_MODEL)

    p = jax.nn.softmax(hg_lb_logits.astype(F32), axis=1)
    lbs = jnp.maximum(jnp.cumsum(p, axis=1) - p[:, :1], 0.0)

    rope = _rope_table(n_lat, n_ctx)
    seg32 = _block_diag_ones(256, 32)
    seg64 = _block_diag_ones(256, 64)
    ones_sub = _block_diag_ones(HG_BLOCK, HG_SUB)
    ii = np.arange(HG_BLOCK)
    same = (ii[:, None] // HG_SUB) == (ii[None, :] // HG_SUB)
    tri = jnp.asarray(np.stack([same & (ii[None, :] <= ii[:, None]), same & (ii[None, :] >= ii[:, None])])
                      .astype(np.float32), dtype=BF16)
    segh = jnp.asarray((np.arange(HG_FDIM)[:, None] // HG_EXPAND == np.arange(GROUP_WIDTH)[None, :] // HG_VDIM)
                       .astype(np.float32), dtype=BF16)
    n_steps = HG_BLOCK // HG_SUB
    segg_np = np.zeros((2, n_steps, HG_BLOCK), np.float32)
    segg_np[0, :, :HG_BLOCK // 2] = np.arange(n_steps)[:, None] == np.arange(HG_BLOCK // 2)[None, :] // (HG_SUB // 2)
    segg_np[1] = np.arange(n_steps)[:, None] == ii[None, :] // HG_SUB
    segg = jnp.asarray(segg_np, dtype=BF16)
    pp = np.arange(HG_SUB)
    negm_np = np.where(np.stack([pp[None, :] <= pp[:, None], pp[None, :] >= pp[:, None]]), 0.0, -np.inf)
    negm = jnp.asarray(np.broadcast_to(negm_np[..., None], (2, HG_SUB, HG_SUB, LANES)).astype(np.float32))

    def group_cols(w, first):
        r = w.shape[0]
        w3 = w.reshape(r, MLA_HEADS, -1)
        return jnp.concatenate([w3[:, :, :first].reshape(r, -1), w3[:, :, first:].reshape(r, -1)], axis=1)

    def fast_flag(qn2, kn2, scale):
        bound = jnp.sqrt(qn2 * kn2) * (scale * LOG2E * 1.02)
        return (bound <= FAST_SCORE_BOUND).astype(jnp.int32).reshape(1)

    def gmax2(g):
        return jnp.max(jnp.abs(g.astype(F32))) ** 2

    for l in range(depth):
        modsel = jnp.stack([mods[l, 0:bsz], jnp.broadcast_to(mods[l, bsz], (bsz, N_MOD, D_MODEL))], axis=1)
        nw = norm_w[l].astype(F32)
        last = l == depth - 1

        xs = _ffn(xs, ctx.astype(F32) if l == 0 else None, modsel, nw,
                  ffn_w_in[l, 0].astype(BF16), ffn_w_out[l, 0].astype(BF16),
                  sub=0, n_lat_tiles=n_lat_tiles, n_rows=t)

        wl = mix_w_in[l]
        w_in = jnp.concatenate(
            [wl[:, 0:A_COLS], wl[:, A_COLS:A_COLS + B_COLS], jnp.zeros((D_MODEL, B_PAD - B_COLS), F32),
             wl[:, A_COLS + B_COLS:]], axis=1).astype(BF16)
        pvec = jnp.concatenate([
            _tile_lanes(da_qk_norm[l, 0], 8), _tile_lanes(da_qk_norm[l, 1], 8),
            mla_q_norm[l].astype(F32), mla_kv_norm[l].astype(F32),
            _tile_lanes(mla_nope_norm[l, 0], 4), _tile_lanes(mla_nope_norm[l, 1], 4),
            _tile_lanes(mla_rope_norm[l, 0], 4), _tile_lanes(mla_rope_norm[l, 1], 4),
            _tile_lanes(gq_qk_norm[l, 0], 4), _tile_lanes(gq_qk_norm[l, 1], 2)]).reshape(1, P_LEN)
        qa, ka, va, qb, kb, vb, qd, kd, vd, zc = _prep(
            xs, modsel, nw, w_in, rope, seg32, seg64, pvec,
            group_cols(mla_w_uq[l], MLA_NOPE_DIM).astype(BF16), group_cols(mla_w_ukv[l], MLA_NOPE_DIM).astype(BF16),
            n_lat_tiles=n_lat_tiles)

        fa = fast_flag(DA_QK_DIM * gmax2(da_qk_norm[l, 0]), DA_QK_DIM * gmax2(da_qk_norm[l, 1]), DA_QK_DIM ** -0.5)
        fb = fast_flag(MLA_NOPE_DIM * gmax2(mla_nope_norm[l, 0]) + MLA_ROPE_DIM * gmax2(mla_rope_norm[l, 0]),
                       MLA_NOPE_DIM * gmax2(mla_nope_norm[l, 1]) + MLA_ROPE_DIM * gmax2(mla_rope_norm[l, 1]),
                       (MLA_NOPE_DIM + MLA_ROPE_DIM) ** -0.5)
        fd = fast_flag(HEAD_DIM * gmax2(gq_qk_norm[l, 0]), HEAD_DIM * gmax2(gq_qk_norm[l, 1]), HEAD_DIM ** -0.5)
        oa = _attend(fa, qa, ka, va, n_lat=n_lat, n_ctx=n_ctx, need_ctx=not last)
        ob = _attend(fb, qb, kb, vb, n_lat=n_lat, n_ctx=n_ctx, need_ctx=not last)
        od = _attend(fd, qd, kd, vd, n_lat=n_lat, n_ctx=n_ctx, need_ctx=not last)

        lb = lbs[:, l]
        lbp = jnp.zeros((2, 8, HG_FDIM), F32)
        lbp = lbp.at[:, 0].set(jnp.log2(lb)).at[:, 1].set(jnp.log1p(-lb) * LOG2E).at[:, 2].set(1.0 - lb)
        oc = _hgrn(zc, lbp, tri, negm, ones_sub, segh, segg, n_lat=n_lat, n_ctx=n_ctx)

        lam_init = 0.8 - 0.6 * math.exp(-0.3 * l)
        lp = da_lambda[l].astype(F32)
        lam = jnp.exp(jnp.sum(lp[0] * lp[1])) - jnp.exp(jnp.sum(lp[2] * lp[3])) + lam_init
        subln = jnp.concatenate([da_subln[l].astype(F32) * (1.0 - lam_init), jnp.zeros((HEAD_DIM,), F32)])
        opv = jnp.concatenate([jnp.tile(subln, DA_HEADS), _tile_lanes(hg_out_norm[l], HG_HEADS),
                               jnp.full((LANES,), lam, F32)]).reshape(1, O_LEN)
        wo = mix_w_out[l]
        wa = jnp.concatenate([wo[0:GROUP_WIDTH].reshape(DA_HEADS, HEAD_DIM, D_MODEL),
                              jnp.zeros((DA_HEADS, HEAD_DIM, D_MODEL), F32)], axis=1).reshape(2 * GROUP_WIDTH, D_MODEL)
        w_out = jnp.concatenate([wa, wo[GROUP_WIDTH:]], axis=0).astype(BF16)
        xs = _out_proj(xs, modsel, oa, ob, od, oc, zc, w_out, opv, seg64, n_lat_tiles=n_lat_tiles,
                       n_rows=n_lat if last else t)

        xs = _ffn(xs, None, modsel, nw, ffn_w_in[l, 1].astype(BF16), ffn_w_out[l, 1].astype(BF16),
                  sub=2, n_lat_tiles=n_lat_tiles, n_rows=n_lat if last else t)
    return xs
```

```python
import functools
import math

import numpy as np
import jax
import jax.numpy as jnp
from jax import lax
from jax.experimental import pallas as pl
from jax.experimental.pallas import tpu as pltpu

F32 = jnp.float32
BF16 = jnp.bfloat16

D_MODEL = 1024
HEAD_DIM = 64
GROUP_WIDTH = D_MODEL // 4
EPS = 1e-6
ROPE_THETA = 10000.0
GRID_W = 64
HALF = 0.5
N_MOD = 9
D_FF = 2816

DA_HEADS = 4
DA_QK_DIM = 32
MLA_HEADS = 4
MLA_Q_RANK = 256
MLA_KV_RANK = 128
MLA_NOPE_DIM = 64
MLA_ROPE_DIM = 32
HG_HEADS = 4
HG_EXPAND = 128
HG_FDIM = 512
HG_VDIM = 64
GQ_HEADS = 4
GQ_KV_HEADS = 2

A_COLS = 768
B_COLS = 416
C_COLS = 2048
D_COLS = 512
B_PAD = 512
W_COLS = A_COLS + B_PAD + C_COLS + D_COLS

LANES = 128
ROW_TILE = 512
ATT_TQ = 2048
ATT_HEADS = 4
ATT_TK_MAX = 2048
V_ROWS = 128
LOG2E = 1.4426950408889634
FAST_SCORE_BOUND = 64.0
HG_BLOCK = 256
HG_SUB = 16
HG_TGROUP = 8
FF_CHUNK = 512
VMEM_LIMIT = 56 * 1024 * 1024

P_GAQ, P_GAK, P_QN, P_KVN, P_NOPE0, P_NOPE1, P_ROPE0, P_ROPE1, P_GDQ, P_GDK, P_LEN = (
    0, 256, 512, 768, 896, 1152, 1408, 1536, 1664, 1920, 2048)
O_SUBLN, O_HGN, O_LAM, O_LEN = 0, 512, 768, 896


def _cparams(sem):
    return pltpu.CompilerParams(dimension_semantics=sem, vmem_limit_bytes=VMEM_LIMIT)


def _resident(shape):
    nd = len(shape)
    return pl.BlockSpec(shape, lambda *_: (0,) * nd, pipeline_mode=pl.Buffered(1))


def _dot(a, b):
    return jnp.dot(a, b, preferred_element_type=F32)


def _rms_rows(x):
    return x * lax.rsqrt(jnp.mean(x * x, axis=-1, keepdims=True) + EPS)


def _seg_norm(x, seg, size):
    ss = _dot((x * x).astype(BF16), seg)
    return x * lax.rsqrt(ss * (1.0 / size) + EPS)


def _swap_halves(x, half):
    lane = lax.broadcasted_iota(jnp.int32, x.shape, 1)
    up = pltpu.roll(x, LANES - half, 1)
    down = pltpu.roll(x, half, 1)
    return jnp.where((lane % (2 * half)) < half, up, down)


def _rope(x, cos, sin, half):
    parts = []
    for c in range(x.shape[1] // LANES):
        xs = x[:, c * LANES:(c + 1) * LANES]
        parts.append(xs * cos + _swap_halves(xs, half) * sin)
    return parts[0] if len(parts) == 1 else jnp.concatenate(parts, axis=1)


def _silu(x):
    return x / (1.0 + jnp.exp(-x))


def _mod_kernel(c_ref, w_ref, b_ref, o_ref):
    a = _silu(c_ref[...])
    a_hi = a.astype(BF16)
    a_lo = (a - a_hi.astype(F32)).astype(BF16)
    w = w_ref[...]
    w_hi = w.astype(BF16)
    w_lo = (w - w_hi.astype(F32)).astype(BF16)
    o_ref[...] = _dot(a_hi, w_hi) + _dot(a_hi, w_lo) + _dot(a_lo, w_hi) + b_ref[...]


def _modulation(cc, ada_w, ada_b):
    depth = ada_w.shape[0]
    nb = N_MOD * D_MODEL // D_MODEL
    return pl.pallas_call(
        _mod_kernel,
        grid=(depth, nb),
        in_specs=[
            pl.BlockSpec((8, D_MODEL), lambda l, j: (0, 0)),
            pl.BlockSpec((None, D_MODEL, D_MODEL), lambda l, j: (l, 0, j)),
            pl.BlockSpec((None, 1, D_MODEL), lambda l, j: (l, 0, j)),
        ],
        out_specs=pl.BlockSpec((None, 8, D_MODEL), lambda l, j: (l, 0, j)),
        out_shape=jax.ShapeDtypeStruct((depth, 8, N_MOD * D_MODEL), F32),
        compiler_params=_cparams(("parallel", "parallel")),
        name="adaln_mod",
    )(cc, ada_w, ada_b.reshape(depth, 1, N_MOD * D_MODEL))


def _ffn_rows(x, m, nw_ref, win_ref, wout_ref, sub):
    shift, scale, gate = m[3 * sub:3 * sub + 1], m[3 * sub + 1:3 * sub + 2], m[3 * sub + 2:3 * sub + 3]
    h = (_rms_rows(x) * nw_ref[sub:sub + 1]) * (1.0 + scale) + shift
    hb = h.astype(BF16)
    acc = jnp.zeros(x.shape, F32)
    for lo in range(0, D_FF, FF_CHUNK):
        hi = min(lo + FF_CHUNK, D_FF)
        g = _dot(hb, win_ref[:, lo:hi])
        u = _dot(hb, win_ref[:, D_FF + lo:D_FF + hi])
        a = (_silu(g) * u).astype(BF16)
        acc = acc + _dot(a, wout_ref[lo:hi, :])
    return x + HALF * gate * acc


def _ffn_kernel(x_ref, c_ref, mod_ref, nw_ref, win_ref, wout_ref, o_ref, *, sub, n_lat_tiles):
    x = x_ref[...]
    if c_ref is not None:
        x = jnp.where(pl.program_id(1) >= n_lat_tiles, c_ref[...], x)
    o_ref[...] = _ffn_rows(x, mod_ref[...], nw_ref, win_ref, wout_ref, sub)


def _ffn(xs, ctx, modsel, nw, win, wout, *, sub, n_lat_tiles, n_rows):
    bsz = xs.shape[0]
    if ctx is None:
        kern = lambda x_ref, *rest: _ffn_kernel(x_ref, None, *rest, sub=sub, n_lat_tiles=n_lat_tiles)
        x_specs = [pl.BlockSpec((None, ROW_TILE, D_MODEL), lambda b, i: (b, i, 0))]
        x_args = [xs]
    else:
        kern = functools.partial(_ffn_kernel, sub=sub, n_lat_tiles=n_lat_tiles)
        x_specs = [pl.BlockSpec((None, ROW_TILE, D_MODEL), lambda b, i: (b, jnp.minimum(i, n_lat_tiles - 1), 0)),
                   pl.BlockSpec((None, ROW_TILE, D_MODEL), lambda b, i: (b, jnp.maximum(i - n_lat_tiles, 0), 0))]
        x_args = [xs, ctx]
    return pl.pallas_call(
        kern,
        grid=(bsz, pl.cdiv(n_rows, ROW_TILE)),
        in_specs=x_specs + [
            pl.BlockSpec((None, None, N_MOD, D_MODEL),
                         lambda b, i: (b, jnp.where(i >= n_lat_tiles, 1, 0), 0, 0)),
            _resident((3, D_MODEL)),
            _resident((D_MODEL, 2 * D_FF)),
            _resident((D_FF, D_MODEL)),
        ],
        out_specs=pl.BlockSpec((None, ROW_TILE, D_MODEL), lambda b, i: (b, i, 0)),
        out_shape=jax.ShapeDtypeStruct((bsz, n_rows, D_MODEL), F32),
        compiler_params=_cparams(("parallel", "parallel")),
        name=f"ffn_half_{sub}",
    )(*x_args, modsel, nw, win, wout)


def _prep_kernel(x_ref, mod_ref, nw_ref, w_ref, rope_ref, seg32_ref, seg64_ref, pv_ref, wuq_ref, wukv_ref,
                 qa_ref, ka_ref, va_ref, qb_ref, kb_ref, vb_ref, qd_ref, kd_ref, vd_ref, zc_ref):
    x = x_ref[...]
    m = mod_ref[...]
    h = (_rms_rows(x) * nw_ref[1:2]) * (1.0 + m[4:5]) + m[3:4]
    hb = h.astype(BF16)
    seg32 = seg32_ref[...]
    seg64 = seg64_ref[...]
    cos32, sin32 = rope_ref[:, 0:128], rope_ref[:, 128:256]
    cos64, sin64 = rope_ref[:, 256:384], rope_ref[:, 384:512]

    def pv(lo, hi):
        return pv_ref[:, lo:hi]

    za = _dot(hb, w_ref[:, 0:A_COLS])
    qa = _rope(_seg_norm(za[:, 0:256], seg32, 32) * pv(P_GAQ, P_GAQ + 256), cos32, sin32, 16)
    qa = qa * (DA_QK_DIM ** -0.5 * LOG2E)
    ka = _rope(_seg_norm(za[:, 256:512], seg32, 32) * pv(P_GAK, P_GAK + 256), cos32, sin32, 16)
    qa_t = jnp.transpose(qa)
    for hh in range(2 * DA_HEADS):
        qa_ref[hh] = qa_t[32 * hh:32 * hh + 32, :].astype(BF16)
        ka_ref[hh] = ka[:, 32 * hh:32 * hh + 32].astype(BF16)

    def put_v(ref, hh, val_t):
        ref[hh, 0:HEAD_DIM, :] = val_t.astype(BF16)
        if V_ROWS > HEAD_DIM:
            ref[hh, HEAD_DIM:V_ROWS, :] = jnp.zeros((V_ROWS - HEAD_DIM, val_t.shape[1]), BF16)

    va_t = jnp.transpose(za[:, 512:768])
    for hh in range(DA_HEADS):
        put_v(va_ref, hh, va_t[64 * hh:64 * hh + 64, :])

    zb = _dot(hb, w_ref[:, A_COLS:A_COLS + B_PAD])
    cq = _rms_rows(zb[:, 0:256]) * pv(P_QN, P_QN + 256)
    q2 = _dot(cq.astype(BF16), wuq_ref[...])
    ckv = _rms_rows(zb[:, 256:384]) * pv(P_KVN, P_KVN + 128)
    kv2 = _dot(ckv.astype(BF16), wukv_ref[...])
    sb = (MLA_NOPE_DIM + MLA_ROPE_DIM) ** -0.5 * LOG2E
    qn = _seg_norm(q2[:, 0:256], seg64, 64) * pv(P_NOPE0, P_NOPE0 + 256) * sb
    kn = _seg_norm(kv2[:, 0:256], seg64, 64) * pv(P_NOPE1, P_NOPE1 + 256)
    seg32s = seg32[0:128, 0:128]
    qr = _rope(_seg_norm(q2[:, 256:384], seg32s, 32) * pv(P_ROPE0, P_ROPE0 + 128), cos32, sin32, 16) * sb
    kr = _rope(_seg_norm(zb[:, 384:512], seg32s, 32) * pv(P_ROPE1, P_ROPE1 + 128), cos32, sin32, 16)
    zpad = jnp.zeros((x.shape[0], 32), BF16)
    zpad_t = jnp.zeros((32, x.shape[0]), BF16)
    qn_t = jnp.transpose(qn)
    qr_t = jnp.transpose(qr)
    vb_t = jnp.transpose(kv2[:, 256:512])
    for hh in range(MLA_HEADS):
        qb_ref[hh, 0:64, :] = qn_t[64 * hh:64 * hh + 64, :].astype(BF16)
        qb_ref[hh, 64:96, :] = qr_t[32 * hh:32 * hh + 32, :].astype(BF16)
        qb_ref[hh, 96:128, :] = zpad_t
        kb_ref[hh, :, 0:64] = kn[:, 64 * hh:64 * hh + 64].astype(BF16)
        kb_ref[hh, :, 64:96] = kr[:, 0:32].astype(BF16)
        kb_ref[hh, :, 96:128] = zpad
        put_v(vb_ref, hh, vb_t[64 * hh:64 * hh + 64, :])

    zc_ref[...] = _dot(hb, w_ref[:, A_COLS + B_PAD:A_COLS + B_PAD + C_COLS])

    zd = _dot(hb, w_ref[:, A_COLS + B_PAD + C_COLS:W_COLS])
    qd = _rope(_seg_norm(zd[:, 0:256], seg64, 64) * pv(P_GDQ, P_GDQ + 256), cos64, sin64, 32)
    qd = qd * (HEAD_DIM ** -0.5 * LOG2E)
    kd = _rope(_seg_norm(zd[:, 256:384], seg64[0:128, 0:128], 64) * pv(P_GDK, P_GDK + 128), cos64, sin64, 32)
    qd_t = jnp.transpose(qd)
    vd_t = jnp.transpose(zd[:, 384:512])
    for hh in range(GQ_HEADS):
        qd_ref[hh] = qd_t[64 * hh:64 * hh + 64, :].astype(BF16)
    for hh in range(GQ_KV_HEADS):
        kd_ref[hh] = kd[:, 64 * hh:64 * hh + 64].astype(BF16)
        put_v(vd_ref, hh, vd_t[64 * hh:64 * hh + 64, :])


def _prep(xs, modsel, nw, w, rope, seg32, seg64, pvec, wuq, wukv, *, n_lat_tiles):
    bsz, t, _ = xs.shape
    n_tiles = pl.cdiv(t, ROW_TILE)

    def heads(h, d):
        return (pl.BlockSpec((None, h, ROW_TILE, d), lambda b, i: (b, 0, i, 0)),
                jax.ShapeDtypeStruct((bsz, h, t, d), BF16))

    def heads_t(h, d):
        return (pl.BlockSpec((None, h, d, ROW_TILE), lambda b, i: (b, 0, 0, i)),
                jax.ShapeDtypeStruct((bsz, h, d, t), BF16))

    outs = [heads_t(8, 32), heads(8, 32), heads_t(4, V_ROWS),
            heads_t(4, 128), heads(4, 128), heads_t(4, V_ROWS),
            heads_t(4, 64), heads(2, 64), heads_t(2, V_ROWS),
            (pl.BlockSpec((None, ROW_TILE, C_COLS), lambda b, i: (b, i, 0)),
             jax.ShapeDtypeStruct((bsz, t, C_COLS), F32))]
    return pl.pallas_call(
        _prep_kernel,
        grid=(bsz, n_tiles),
        in_specs=[
            pl.BlockSpec((None, ROW_TILE, D_MODEL), lambda b, i: (b, i, 0)),
            pl.BlockSpec((None, None, N_MOD, D_MODEL),
                         lambda b, i: (b, jnp.where(i >= n_lat_tiles, 1, 0), 0, 0)),
            _resident((3, D_MODEL)),
            _resident((D_MODEL, W_COLS)),
            pl.BlockSpec((ROW_TILE, 512), lambda b, i: (i, 0)),
            _resident((256, 256)),
            _resident((256, 256)),
            _resident((1, P_LEN)),
            _resident((MLA_Q_RANK, 384)),
            _resident((MLA_KV_RANK, 512)),
        ],
        out_specs=[o[0] for o in outs],
        out_shape=[o[1] for o in outs],
        compiler_params=_cparams(("parallel", "parallel")),
        name="mix_in_prep",
    )(xs, modsel, nw, w, rope, seg32, seg64, pvec, wuq, wukv)


def _attn_kernel(flag_ref, q_ref, k_ref, v_ref, kc_ref, vc_ref, o_ref, acc_scr, l_scr, m_scr, *, hq, hk, hv, n_kv):
    j = pl.program_id(3)
    fast = flag_ref[0] == 1

    @pl.when(j == 0)
    def _():
        acc_scr[...] = jnp.zeros(acc_scr.shape, F32)
        l_scr[...] = jnp.zeros(l_scr.shape, F32)
        m_scr[...] = jnp.full(m_scr.shape, -jnp.inf, F32)

    def scores(h, kr):
        return _dot(kr[h // (hq // hk)], q_ref[h])

    def step_fast(kr, vr):
        for h in range(hq):
            p = jnp.exp2(scores(h, kr))
            l_scr[h] += jnp.sum(p, axis=0, keepdims=True)
            acc_scr[h] += _dot(vr[h // (hq // hv)], p.astype(BF16))

    def step_safe(kr, vr):
        for h in range(hq):
            s = scores(h, kr)
            m_prev = m_scr[h]
            m_new = jnp.maximum(m_prev, jnp.max(s, axis=0, keepdims=True))
            alpha = jnp.exp2(m_prev - m_new)
            p = jnp.exp2(s - m_new[0:1, :])
            l_scr[h] = alpha * l_scr[h] + jnp.sum(p, axis=0, keepdims=True)
            acc_scr[h] = alpha[0:1, :] * acc_scr[h] + _dot(vr[h // (hq // hv)], p.astype(BF16))
            m_scr[h] = m_new

    def both(kr, vr):
        @pl.when(fast)
        def _():
            step_fast(kr, vr)

        @pl.when(jnp.logical_not(fast))
        def _():
            step_safe(kr, vr)

    @pl.when(j < n_kv)
    def _():
        both(k_ref, v_ref)

    @pl.when(j == n_kv)
    def _():
        both(kc_ref, vc_ref)
        outs = []
        for h in range(hq):
            outs.append(acc_scr[h][0:HEAD_DIM, :] / l_scr[h][0:1, :])
        o_ref[...] = jnp.transpose(jnp.concatenate(outs, axis=0))


def _att_key_tile(hq, n_lat):
    tk = ATT_TK_MAX
    while tk > LANES and (hq * ATT_TQ * tk * 6 > VMEM_LIMIT or n_lat % tk):
        tk //= 2
    return tk


def _attention(flag, q, k, v, *, n_lat, n_ctx, latent):
    bsz, hq_all, dq, t = q.shape
    n_grp = hq_all // ATT_HEADS
    hq = ATT_HEADS
    hk, hv = k.shape[1] // n_grp, v.shape[1] // n_grp
    width = hq * HEAD_DIM
    ctx_blk = n_lat // n_ctx
    tk = _att_key_tile(hq, n_lat)
    if latent:
        tq, nq, n_kv, q_off = ATT_TQ, n_lat // ATT_TQ, n_lat // tk, 0
    else:
        tq, nq, n_kv, q_off = n_ctx, 1, 0, ctx_blk

    def kv_blk(j):
        return jnp.minimum(j, max(n_kv - 1, 0))

    in_specs = [
        pl.BlockSpec((None, hq, dq, tq), lambda b, g, i, j, f: (b, g, 0, i + q_off)),
        pl.BlockSpec((None, hk, tk, dq), lambda b, g, i, j, f: (b, g, kv_blk(j), 0)),
        pl.BlockSpec((None, hv, V_ROWS, tk), lambda b, g, i, j, f: (b, g, 0, kv_blk(j))),
        pl.BlockSpec((None, hk, n_ctx, dq), lambda b, g, i, j, f: (b, g, ctx_blk, 0)),
        pl.BlockSpec((None, hv, V_ROWS, n_ctx), lambda b, g, i, j, f: (b, g, 0, ctx_blk)),
    ]
    grid_spec = pltpu.PrefetchScalarGridSpec(
        num_scalar_prefetch=1,
        grid=(bsz, n_grp, nq, n_kv + 1),
        in_specs=in_specs,
        out_specs=pl.BlockSpec((None, tq, width), lambda b, g, i, j, f: (b, i, g)),
        scratch_shapes=[pltpu.VMEM((hq, V_ROWS, tq), F32), pltpu.VMEM((hq, 8, tq), F32),
                        pltpu.VMEM((hq, 8, tq), F32)],
    )
    return pl.pallas_call(
        functools.partial(_attn_kernel, hq=hq, hk=hk, hv=hv, n_kv=n_kv),
        grid_spec=grid_spec,
        out_shape=jax.ShapeDtypeStruct((bsz, nq * tq, hq_all * HEAD_DIM), F32),
        compiler_params=_cparams(("parallel", "parallel", "parallel", "arbitrary")),
        name=f"flash_attn_h{hq_all}_d{dq}_{'lat' if latent else 'ctx'}",
    )(flag, q, k, v, k, v)


def _attend(flag, q, k, v, *, n_lat, n_ctx, need_ctx):
    o_lat = _attention(flag, q, k, v, n_lat=n_lat, n_ctx=n_ctx, latent=True)
    o_ctx = _attention(flag, q, k, v, n_lat=n_lat, n_ctx=n_ctx, latent=False) if need_ctx else None
    return o_lat, o_ctx


def _hgrn_block(zq_ref, zx_ref, zi_ref, lb_ref, tri_ref, negm_ref, ones_ref, segh_ref, segg_ref, o_ref, st_scr, oi_scr,
                *, rev):
    ng = HG_BLOCK // HG_SUB
    x2 = zx_ref[...] * LOG2E
    lb = lb_ref[...]
    log_lb, log_1m, one_m = lb[0:1], lb[1:2], lb[2:3]
    e = jnp.exp2(-jnp.abs(x2))
    rcp = 1.0 / (1.0 + e)
    log_sig = jnp.minimum(x2, 0.0) - jnp.log2(1.0 + e)
    b = log_1m + log_sig
    big = jnp.maximum(log_lb, b)
    logf = big + jnp.log2(1.0 + jnp.exp2(-jnp.abs(log_lb - b)))
    kk = one_m * (jnp.where(x2 >= 0.0, e, 1.0) * rcp)
    zq = zq_ref[...]
    q = zq / (1.0 + jnp.exp2(zq * -LOG2E))
    v = zi_ref[:, 0:GROUP_WIDTH]

    h1 = logf.astype(BF16)
    r1 = logf - h1.astype(F32)
    h2 = r1.astype(BF16)
    h3 = (r1 - h2.astype(F32)).astype(BF16)
    tri = tri_ref[...]
    ones = ones_ref[...]
    g = _dot(tri, h1) + _dot(tri, h2) + _dot(tri, h3)
    gtot = _dot(ones, h1) + _dot(ones, h2) + _dot(ones, h3)
    qdec = jnp.exp2(g)
    kdec = jnp.exp2(gtot - g)
    dec = qdec * kdec
    qtb = (q * qdec).astype(BF16)
    ktb = (kk * kdec).astype(BF16)
    vb = v.astype(BF16)

    heads = [(slice(HG_EXPAND * hh, HG_EXPAND * (hh + 1)), slice(HG_VDIM * hh, HG_VDIM * (hh + 1)))
             for hh in range(HG_HEADS)]
    upd = []
    for n in range(ng):
        r = slice(n * HG_SUB, (n + 1) * HG_SUB)
        upd.append(jnp.concatenate(
            [lax.dot_general(vb[r, vs], ktb[r, ks], (((0,), (0,)), ((), ())), preferred_element_type=F32)
             for ks, vs in heads], axis=1))
    st = st_scr[...]
    inter = [None] * ng
    for n in (range(ng - 1, -1, -1) if rev else range(ng)):
        r = slice(n * HG_SUB, (n + 1) * HG_SUB)
        stb = st.astype(BF16)
        inter[n] = jnp.concatenate(
            [lax.dot_general(qtb[r, ks], stb[:, ks], (((1,), (1,)), ((), ())), preferred_element_type=F32)
             for ks, _ in heads], axis=1)
        st = st * dec[n * HG_SUB:n * HG_SUB + 1, :] + upd[n]
    st_scr[...] = st

    g3 = g.reshape(ng, HG_SUB, HG_FDIM)
    k3 = kk.reshape(ng, HG_SUB, HG_FDIM)
    q3 = q.reshape(ng, HG_SUB, HG_FDIM)
    v3 = v.reshape(ng, HG_SUB, GROUP_WIDTH)
    segh = segh_ref[...]
    half = HG_SUB // 2

    def span(t):
        if rev:
            return (half, HG_SUB) if t >= half else (0, HG_SUB)
        return (0, half) if t < half else (0, HG_SUB)

    for t0 in range(0, HG_SUB, HG_TGROUP):
        ts = range(t0, t0 + HG_TGROUP)
        wbs = []
        for t in ts:
            s0, s1 = span(t)
            tt = (t // half) * half
            nm = jnp.concatenate([negm_ref[t, tt:tt + half, :]] * (HG_FDIM // LANES), axis=1)[None]
            gt = g3[:, t:t + 1, :]
            parts = [gt - g3[:, a:a + half, :] for a in range(s0, s1, half)]
            parts = [p + nm if a == tt else p for a, p in zip(range(s0, s1, half), parts)]
            diff = parts[0] if len(parts) == 1 else jnp.concatenate(parts, axis=1)
            w = jnp.exp2(diff) * k3[:, s0:s1, :] * q3[:, t:t + 1, :]
            wbs.append(w.reshape(ng * (s1 - s0), HG_FDIM).astype(BF16))
        wb = jnp.concatenate(wbs, axis=0)
        hr = wb.shape[0] // 2
        cexp = jnp.concatenate([_dot(wb[0:hr], segh), _dot(wb[hr:], segh)], axis=0)
        row = 0
        for t in ts:
            s0, s1 = span(t)
            nr = ng * (s1 - s0)
            prod = (cexp[row:row + nr] * v3[:, s0:s1, :].reshape(nr, GROUP_WIDTH)).astype(BF16)
            row += nr
            segg = segg_ref[0 if s1 - s0 == half else 1]
            res = _dot(segg[:, 0:nr], prod)
            for c in range(GROUP_WIDTH // LANES):
                oi_scr[c][pl.ds(t, ng, stride=HG_SUB), :] = res[:, c * LANES:(c + 1) * LANES]
    intra = jnp.concatenate([oi_scr[c][...] for c in range(GROUP_WIDTH // LANES)], axis=1)
    o_ref[...] = jnp.concatenate(inter, axis=0) + intra


def _hgrn_kernel(zq_ref, zx_ref, zi_ref, lb_ref, tri_ref, negm_ref, ones_ref, segh_ref, segg_ref, o_ref, st_scr,
                 *oi_scr):
    d = pl.program_id(1)
    j = pl.program_id(2)

    @pl.when(j == 0)
    def _():
        st_scr[...] = jnp.zeros(st_scr.shape, F32)

    args = (zq_ref, zx_ref, zi_ref, lb_ref, tri_ref, negm_ref, ones_ref, segh_ref, segg_ref, o_ref, st_scr, oi_scr)

    @pl.when(d == 0)
    def _():
        _hgrn_block(*args, rev=False)

    @pl.when(d == 1)
    def _():
        _hgrn_block(*args, rev=True)


def _hgrn(zc, lbp, tri, negm, ones, segh, segg, *, n_lat, n_ctx):
    bsz, t, _ = zc.shape
    nl = n_lat // HG_BLOCK
    nc = n_ctx // HG_BLOCK

    def blk(d, j):
        ctx = jnp.where(d == 0, nl + j, nl + nc - 1 - j)
        lat = jnp.where(d == 0, j - nc, nl - 1 - (j - nc))
        return jnp.where(j < nc, ctx, lat)

    return pl.pallas_call(
        _hgrn_kernel,
        grid=(bsz, 2, nl + nc),
        in_specs=[
            pl.BlockSpec((None, HG_BLOCK, HG_FDIM), lambda b, d, j: (b, blk(d, j), 0)),
            pl.BlockSpec((None, HG_BLOCK, HG_FDIM), lambda b, d, j: (b, blk(d, j), 1 + d)),
            pl.BlockSpec((None, HG_BLOCK, HG_FDIM), lambda b, d, j: (b, blk(d, j), 3)),
            pl.BlockSpec((None, 8, HG_FDIM), lambda b, d, j: (d, 0, 0)),
            pl.BlockSpec((None, HG_BLOCK, HG_BLOCK), lambda b, d, j: (d, 0, 0)),
            pl.BlockSpec((None, HG_SUB, HG_SUB, LANES), lambda b, d, j: (d, 0, 0, 0)),
            _resident((HG_BLOCK, HG_BLOCK)),
            _resident((HG_FDIM, GROUP_WIDTH)),
            _resident((2, HG_BLOCK // HG_SUB, HG_BLOCK)),
        ],
        out_specs=pl.BlockSpec((None, None, HG_BLOCK, GROUP_WIDTH), lambda b, d, j: (b, d, blk(d, j), 0)),
        out_shape=jax.ShapeDtypeStruct((bsz, 2, t, GROUP_WIDTH), F32),
        scratch_shapes=[pltpu.VMEM((HG_VDIM, HG_FDIM), F32)]
        + [pltpu.VMEM((HG_BLOCK, LANES), F32) for _ in range(GROUP_WIDTH // LANES)],
        compiler_params=_cparams(("parallel", "parallel", "arbitrary")),
        name="hgrn2_scan",
    )(zc, zc, zc, lbp, tri, negm, ones, segh, segg)


def _out_kernel(*refs, n_lat_tiles, with_ctx):
    if with_ctx:
        (x_ref, mod_ref, oa_ref, ob_ref, od_ref, oa_c_ref, ob_c_ref, od_c_ref,
         ocf_ref, ocb_ref, g_ref, w_ref, pv_ref, seg64_ref, o_ref) = refs
        is_ctx = pl.program_id(1) >= n_lat_tiles
        oa = jnp.where(is_ctx, oa_c_ref[...], oa_ref[...])
        ob = jnp.where(is_ctx, ob_c_ref[...], ob_ref[...])
        od = jnp.where(is_ctx, od_c_ref[...], od_ref[...])
    else:
        x_ref, mod_ref, oa_ref, ob_ref, od_ref, ocf_ref, ocb_ref, g_ref, w_ref, pv_ref, seg64_ref, o_ref = refs
        oa, ob, od = oa_ref[...], ob_ref[...], od_ref[...]
    x = x_ref[...]
    m = mod_ref[...]
    seg64 = seg64_ref[...]
    lam = pv_ref[:, O_LAM:O_LAM + LANES]

    lane = lax.broadcasted_iota(jnp.int32, (x.shape[0], LANES), 1)
    ya = []
    for c in range(DA_HEADS):
        pair = oa[:, c * LANES:(c + 1) * LANES]
        diff = pair - lam * pltpu.roll(pair, HEAD_DIM, 1)
        ya.append(jnp.where(lane < HEAD_DIM, diff, 0.0))
    ya = jnp.concatenate(ya, axis=1)
    ya = jnp.concatenate([_seg_norm(ya[:, 0:256], seg64, 64), _seg_norm(ya[:, 256:512], seg64, 64)], axis=1)
    ya = ya * pv_ref[:, O_SUBLN:O_SUBLN + 512]

    oc = ocf_ref[...] + ocb_ref[...]
    yc = _seg_norm(oc, seg64, 64) * pv_ref[:, O_HGN:O_HGN + 256] * _silu(g_ref[...])

    acc = _dot(ya.astype(BF16), w_ref[0:512, :])
    acc = acc + _dot(ob.astype(BF16), w_ref[512:768, :])
    acc = acc + _dot(yc.astype(BF16), w_ref[768:1024, :])
    acc = acc + _dot(od.astype(BF16), w_ref[1024:1280, :])
    o_ref[...] = x + m[5:6] * acc


def _out_proj(xs, modsel, oa, ob, od, oc, zc, w, pvec, seg64, *, n_lat_tiles, n_rows):
    bsz = xs.shape[0]
    n_tiles = pl.cdiv(n_rows, ROW_TILE)
    with_ctx = n_tiles > n_lat_tiles
    row = lambda width: pl.BlockSpec((None, ROW_TILE, width), lambda b, i: (b, i, 0))
    lat = lambda width: pl.BlockSpec((None, ROW_TILE, width), lambda b, i: (b, jnp.minimum(i, n_lat_tiles - 1), 0))
    ctx = lambda width: pl.BlockSpec((None, ROW_TILE, width), lambda b, i: (b, jnp.maximum(i - n_lat_tiles, 0), 0))
    att_specs = [lat(512), lat(256), lat(256)]
    att_args = [oa[0], ob[0], od[0]]
    if with_ctx:
        att_specs += [ctx(512), ctx(256), ctx(256)]
        att_args += [oa[1], ob[1], od[1]]
    return pl.pallas_call(
        functools.partial(_out_kernel, n_lat_tiles=n_lat_tiles, with_ctx=with_ctx),
        grid=(bsz, n_tiles),
        in_specs=[
            row(D_MODEL),
            pl.BlockSpec((None, None, N_MOD, D_MODEL),
                         lambda b, i: (b, jnp.where(i >= n_lat_tiles, 1, 0), 0, 0)),
            *att_specs,
            pl.BlockSpec((None, None, ROW_TILE, GROUP_WIDTH), lambda b, i: (b, 0, i, 0)),
            pl.BlockSpec((None, None, ROW_TILE, GROUP_WIDTH), lambda b, i: (b, 1, i, 0)),
            pl.BlockSpec((None, ROW_TILE, GROUP_WIDTH), lambda b, i: (b, i, C_COLS // GROUP_WIDTH - 1)),
            _resident((1280, D_MODEL)),
            _resident((1, O_LEN)),
            _resident((256, 256)),
        ],
        out_specs=row(D_MODEL),
        out_shape=jax.ShapeDtypeStruct((bsz, n_rows, D_MODEL), F32),
        compiler_params=_cparams(("parallel", "parallel")),
        name="mix_out_proj",
    )(xs, modsel, *att_args, oc, oc, zc, w, pvec, seg64)


def _block_diag_ones(n, size):
    idx = np.arange(n) // size
    return jnp.asarray((idx[:, None] == idx[None, :]).astype(np.float32), dtype=BF16)


def _rope_table(n_lat, n_ctx):
    rows = n_lat // GRID_W
    r = jnp.repeat(jnp.arange(rows), GRID_W).astype(F32)
    col = jnp.tile(jnp.arange(GRID_W), rows).astype(F32)

    def pattern(rot_dim):
        n_freq = rot_dim // 4
        inv = ROPE_THETA ** (-jnp.arange(n_freq, dtype=F32) / n_freq)
        ang = jnp.concatenate([r[:, None] * inv, col[:, None] * inv], axis=-1)
        cos, sin = jnp.cos(ang), jnp.sin(ang)
        cos_l = jnp.tile(jnp.concatenate([cos, cos], axis=-1), (1, LANES // rot_dim))
        sin_l = jnp.tile(jnp.concatenate([-sin, sin], axis=-1), (1, LANES // rot_dim))
        cos_l = jnp.concatenate([cos_l, jnp.ones((n_ctx, LANES), F32)], axis=0)
        sin_l = jnp.concatenate([sin_l, jnp.zeros((n_ctx, LANES), F32)], axis=0)
        return cos_l, sin_l

    c32, s32 = pattern(32)
    c64, s64 = pattern(64)
    return jnp.concatenate([c32, s32, c64, s64], axis=1)


def _tile_lanes(v, reps):
    return jnp.tile(v.astype(F32), reps)


def kernel(x, c, ctx, c_ctx, ada_w, ada_b, norm_w, ffn_w_in, ffn_w_out, mix_w_in, mix_w_out, da_qk_norm, da_lambda,
           da_subln, mla_q_norm, mla_kv_norm, mla_w_uq, mla_w_ukv, mla_nope_norm, mla_rope_norm, hg_lb_logits,
           hg_out_norm, gq_qk_norm):
    bsz, n_lat, _ = x.shape
    n_ctx = ctx.shape[1]
    depth = ada_w.shape[0]
    assert bsz + 1 <= 8 and n_lat % ATT_TQ == 0 and n_lat % n_ctx == 0
    assert n_lat % ROW_TILE == 0 and n_ctx % HG_BLOCK == 0
    t = n_lat + n_ctx
    n_lat_tiles = n_lat // ROW_TILE
    xs = x.astype(F32)

    cc =jnp.zeros((8, D_MODEL), F32).at[0:bsz].set(c).at[bsz].set(c_ctx)
    mods = _modulation(cc, ada_w, ada_b).reshape(depth, 8, N_MOD, D_MODEL)

    p = jax.nn.softmax(hg_lb_logits.astype(F32), axis=1)
    lbs = jnp.maximum(jnp.cumsum(p, axis=1) - p[:, :1], 0.0)

    rope = _rope_table(n_lat, n_ctx)
    seg32 = _block_diag_ones(256, 32)
    seg64 = _block_diag_ones(256, 64)
    ones_sub = _block_diag_ones(HG_BLOCK, HG_SUB)
    ii = np.arange(HG_BLOCK)
    same = (ii[:, None] // HG_SUB) == (ii[None, :] // HG_SUB)
    tri = jnp.asarray(np.stack([same & (ii[None, :] <= ii[:, None]), same & (ii[None, :] >= ii[:, None])])
                      .astype(np.float32), dtype=BF16)
    segh = jnp.asarray((np.arange(HG_FDIM)[:, None] // HG_EXPAND == np.arange(GROUP_WIDTH)[None, :] // HG_VDIM)
                       .astype(np.float32), dtype=BF16)
    n_steps = HG_BLOCK // HG_SUB
    segg_np = np.zeros((2, n_steps, HG_BLOCK), np.float32)
    segg_np[0, :, :HG_BLOCK // 2] = np.arange(n_steps)[:, None] == np.arange(HG_BLOCK // 2)[None, :] // (HG_SUB // 2)
    segg_np[1] = np.arange(n_steps)[:, None] == ii[None, :] // HG_SUB
    segg = jnp.asarray(segg_np, dtype=BF16)
    pp = np.arange(HG_SUB)
    negm_np = np.where(np.stack([pp[None, :] <= pp[:, None], pp[None, :] >= pp[:, None]]), 0.0, -np.inf)
    negm = jnp.asarray(np.broadcast_to(negm_np[..., None], (2, HG_SUB, HG_SUB, LANES)).astype(np.float32))

    def group_cols(w, first):
        r = w.shape[0]
        w3 = w.reshape(r, MLA_HEADS, -1)
        return jnp.concatenate([w3[:, :, :first].reshape(r, -1), w3[:, :, first:].reshape(r, -1)], axis=1)

    def fast_flag(qn2, kn2, scale):
        bound = jnp.sqrt(qn2 * kn2) * (scale * LOG2E * 1.02)
        return (bound <= FAST_SCORE_BOUND).astype(jnp.int32).reshape(1)

    def gmax2(g):
        return jnp.max(jnp.abs(g.astype(F32))) ** 2

    for l in range(depth):
        modsel = jnp.stack([mods[l, 0:bsz], jnp.broadcast_to(mods[l, bsz], (bsz, N_MOD, D_MODEL))], axis=1)
        nw = norm_w[l].astype(F32)
        last = l == depth - 1

        xs = _ffn(xs, ctx.astype(F32) if l == 0 else None, modsel, nw,
                  ffn_w_in[l, 0].astype(BF16), ffn_w_out[l, 0].astype(BF16),
                  sub=0, n_lat_tiles=n_lat_tiles, n_rows=t)

        wl = mix_w_in[l]
        w_in = jnp.concatenate(
            [wl[:, 0:A_COLS], wl[:, A_COLS:A_COLS + B_COLS], jnp.zeros((D_MODEL, B_PAD - B_COLS), F32),
             wl[:, A_COLS + B_COLS:]], axis=1).astype(BF16)
        pvec = jnp.concatenate([
            _tile_lanes(da_qk_norm[l, 0], 8), _tile_lanes(da_qk_norm[l, 1], 8),
            mla_q_norm[l].astype(F32), mla_kv_norm[l].astype(F32),
            _tile_lanes(mla_nope_norm[l, 0], 4), _tile_lanes(mla_nope_norm[l, 1], 4),
            _tile_lanes(mla_rope_norm[l, 0], 4), _tile_lanes(mla_rope_norm[l, 1], 4),
            _tile_lanes(gq_qk_norm[l, 0], 4), _tile_lanes(gq_qk_norm[l, 1], 2)]).reshape(1, P_LEN)
        qa, ka, va, qb, kb, vb, qd, kd, vd, zc = _prep(
            xs, modsel, nw, w_in, rope, seg32, seg64, pvec,
            group_cols(mla_w_uq[l], MLA_NOPE_DIM).astype(BF16), group_cols(mla_w_ukv[l], MLA_NOPE_DIM).astype(BF16),
            n_lat_tiles=n_lat_tiles)

        fa = fast_flag(DA_QK_DIM * gmax2(da_qk_norm[l, 0]), DA_QK_DIM * gmax2(da_qk_norm[l, 1]), DA_QK_DIM ** -0.5)
        fb = fast_flag(MLA_NOPE_DIM * gmax2(mla_nope_norm[l, 0]) + MLA_ROPE_DIM * gmax2(mla_rope_norm[l, 0]),
                       MLA_NOPE_DIM * gmax2(mla_nope_norm[l, 1]) + MLA_ROPE_DIM * gmax2(mla_rope_norm[l, 1]),
                       (MLA_NOPE_DIM + MLA_ROPE_DIM) ** -0.5)
        fd = fast_flag(HEAD_DIM * gmax2(gq_qk_norm[l, 0]), HEAD_DIM * gmax2(gq_qk_norm[l, 1]), HEAD_DIM ** -0.5)
        oa = _attend(fa, qa, ka, va, n_lat=n_lat, n_ctx=n_ctx, need_ctx=not last)
        ob = _attend(fb, qb, kb, vb, n_lat=n_lat, n_ctx=n_ctx, need_ctx=not last)
        od = _attend(fd, qd, kd, vd, n_lat=n_lat, n_ctx=n_ctx, need_ctx=not last)

        lb = lbs[:, l]
        lbp = jnp.zeros((2, 8, HG_FDIM), F32)
        lbp = lbp.at[:, 0].set(jnp.log2(lb)).at[:, 1].set(jnp.log1p(-lb) * LOG2E).at[:, 2].set(1.0 - lb)
        oc = _hgrn(zc, lbp, tri, negm, ones_sub, segh, segg, n_lat=n_lat, n_ctx=n_ctx)

        lam_init = 0.8 - 0.6 * math.exp(-0.3 * l)
        lp = da_lambda[l].astype(F32)
        lam = jnp.exp(jnp.sum(lp[0] * lp[1])) - jnp.exp(jnp.sum(lp[2] * lp[3])) + lam_init
        subln = jnp.concatenate([da_subln[l].astype(F32) * (1.0 - lam_init), jnp.zeros((HEAD_DIM,), F32)])
        opv = jnp.concatenate([jnp.tile(subln, DA_HEADS), _tile_lanes(hg_out_norm[l], HG_HEADS),
                               jnp.full((LANES,), lam, F32)]).reshape(1, O_LEN)
        wo = mix_w_out[l]
        wa = jnp.concatenate([wo[0:GROUP_WIDTH].reshape(DA_HEADS, HEAD_DIM, D_MODEL),
                              jnp.zeros((DA_HEADS, HEAD_DIM, D_MODEL), F32)], axis=1).reshape(2 * GROUP_WIDTH, D_MODEL)
        w_out = jnp.concatenate([wa, wo[GROUP_WIDTH:]], axis=0).astype(BF16)
        xs = _out_proj(xs, modsel, oa, ob, od, oc, zc, w_out, opv, seg64, n_lat_tiles=n_lat_tiles,
                       n_rows=n_lat if last else t)

        xs = _ffn(xs, None, modsel, nw, ffn_w_in[l, 1].astype(BF16), ffn_w_out[l, 1].astype(BF16),
                  sub=2, n_lat_tiles=n_lat_tiles, n_rows=n_lat if last else t)
    return xs
```

```python
import functools
import math

import numpy as np
import jax
import jax.numpy as jnp
from jax import lax
from jax.experimental import pallas as pl
from jax.experimental.pallas import tpu as pltpu

F32 = jnp.float32
BF16 = jnp.bfloat16

D_MODEL = 1024
HEAD_DIM = 64
GROUP_WIDTH = D_MODEL // 4
EPS = 1e-6
ROPE_THETA = 10000.0
GRID_W = 64
HALF = 0.5
N_MOD = 9
D_FF = 2816

DA_HEADS = 4
DA_QK_DIM = 32
MLA_HEADS = 4
MLA_Q_RANK = 256
MLA_KV_RANK = 128
MLA_NOPE_DIM = 64
MLA_ROPE_DIM = 32
HG_HEADS = 4
HG_EXPAND = 128
HG_FDIM = 512
HG_VDIM = 64
GQ_HEADS = 4
GQ_KV_HEADS = 2

A_COLS = 768
B_COLS = 416
C_COLS = 2048
D_COLS = 512
B_PAD = 512
W_COLS = A_COLS + B_PAD + C_COLS + D_COLS

LANES = 128
ROW_TILE = 512
ATT_TQ = 2048
ATT_HEADS = 4
ATT_TK_MAX = 2048
V_ROWS = 128
LOG2E = 1.4426950408889634
FAST_SCORE_BOUND = 64.0
HG_BLOCK = 256
HG_SUB = 16
HG_TGROUP = 8
FF_CHUNK = 512
VMEM_LIMIT = 56 * 1024 * 1024

P_GAQ, P_GAK, P_QN, P_KVN, P_NOPE0, P_NOPE1, P_ROPE0, P_ROPE1, P_GDQ, P_GDK, P_LEN = (
    0, 256, 512, 768, 896, 1152, 1408, 1536, 1664, 1920, 2048)
O_SUBLN, O_HGN, O_LAM, O_LEN = 0, 512, 768, 896


def _cparams(sem):
    return pltpu.CompilerParams(dimension_semantics=sem, vmem_limit_bytes=VMEM_LIMIT)


def _resident(shape):
    nd = len(shape)
    return pl.BlockSpec(shape, lambda *_: (0,) * nd, pipeline_mode=pl.Buffered(1))


def _dot(a, b):
    return jnp.dot(a, b, preferred_element_type=F32)


def _rms_rows(x):
    return x * lax.rsqrt(jnp.mean(x * x, axis=-1, keepdims=True) + EPS)


def _seg_norm(x, seg, size):
    ss = _dot((x * x).astype(BF16), seg)
    return x * lax.rsqrt(ss * (1.0 / size) + EPS)


def _swap_halves(x, half):
    lane = lax.broadcasted_iota(jnp.int32, x.shape, 1)
    up = pltpu.roll(x, LANES - half, 1)
    down = pltpu.roll(x, half, 1)
    return jnp.where((lane % (2 * half)) < half, up, down)


def _rope(x, cos, sin, half):
    parts = []
    for c in range(x.shape[1] // LANES):
        xs = x[:, c * LANES:(c + 1) * LANES]
        parts.append(xs * cos + _swap_halves(xs, half) * sin)
    return parts[0] if len(parts) == 1 else jnp.concatenate(parts, axis=1)


def _silu(x):
    return x / (1.0 + jnp.exp(-x))


def _mod_kernel(c_ref, w_ref, b_ref, o_ref):
    a = _silu(c_ref[...])
    a_hi = a.astype(BF16)
    a_lo = (a - a_hi.astype(F32)).astype(BF16)
    w = w_ref[...]
    w_hi = w.astype(BF16)
    w_lo = (w - w_hi.astype(F32)).astype(BF16)
    o_ref[...] = _dot(a_hi, w_hi) + _dot(a_hi, w_lo) + _dot(a_lo, w_hi) + b_ref[...]


def _modulation(cc, ada_w, ada_b):
    depth = ada_w.shape[0]
    nb = N_MOD * D_MODEL // D_MODEL
    return pl.pallas_call(
        _mod_kernel,
        grid=(depth, nb),
        in_specs=[
            pl.BlockSpec((8, D_MODEL), lambda l, j: (0, 0)),
            pl.BlockSpec((None, D_MODEL, D_MODEL), lambda l, j: (l, 0, j)),
            pl.BlockSpec((None, 1, D_MODEL), lambda l, j: (l, 0, j)),
        ],
        out_specs=pl.BlockSpec((None, 8, D_MODEL), lambda l, j: (l, 0, j)),
        out_shape=jax.ShapeDtypeStruct((depth, 8, N_MOD * D_MODEL), F32),
        compiler_params=_cparams(("parallel", "parallel")),
        name="adaln_mod",
    )(cc, ada_w, ada_b.reshape(depth, 1, N_MOD * D_MODEL))


def _ffn_rows(x, m, nw_ref, win_ref, wout_ref, sub):
    shift, scale, gate = m[3 * sub:3 * sub + 1], m[3 * sub + 1:3 * sub + 2], m[3 * sub + 2:3 * sub + 3]
    h = (_rms_rows(x) * nw_ref[sub:sub + 1]) * (1.0 + scale) + shift
    hb = h.astype(BF16)
    acc = jnp.zeros(x.shape, F32)
    for lo in range(0, D_FF, FF_CHUNK):
        hi = min(lo + FF_CHUNK, D_FF)
        g = _dot(hb, win_ref[:, lo:hi])
        u = _dot(hb, win_ref[:, D_FF + lo:D_FF + hi])
        a = (_silu(g) * u).astype(BF16)
        acc = acc + _dot(a, wout_ref[lo:hi, :])
    return x + HALF * gate * acc


def _ffn_kernel(x_ref, c_ref, mod_ref, nw_ref, win_ref, wout_ref, o_ref, *, sub, n_lat_tiles):
    x = x_ref[...]
    if c_ref is not None:
        x = jnp.where(pl.program_id(1) >= n_lat_tiles, c_ref[...], x)
    o_ref[...] = _ffn_rows(x, mod_ref[...], nw_ref, win_ref, wout_ref, sub)


def _ffn(xs, ctx, modsel, nw, win, wout, *, sub, n_lat_tiles, n_rows):
    bsz = xs.shape[0]
    if ctx is None:
        kern = lambda x_ref, *rest: _ffn_kernel(x_ref, None, *rest, sub=sub, n_lat_tiles=n_lat_tiles)
        x_specs = [pl.BlockSpec((None, ROW_TILE, D_MODEL), lambda b, i: (b, i, 0))]
        x_args = [xs]
    else:
        kern = functools.partial(_ffn_kernel, sub=sub, n_lat_tiles=n_lat_tiles)
        x_specs = [pl.BlockSpec((None, ROW_TILE, D_MODEL), lambda b, i: (b, jnp.minimum(i, n_lat_tiles - 1), 0)),
                   pl.BlockSpec((None, ROW_TILE, D_MODEL), lambda b, i: (b, jnp.maximum(i - n_lat_tiles, 0), 0))]
        x_args = [xs, ctx]
    return pl.pallas_call(
        kern,
        grid=(bsz, pl.cdiv(n_rows, ROW_TILE)),
        in_specs=x_specs + [
            pl.BlockSpec((None, None, N_MOD, D_MODEL),
                         lambda b, i: (b, jnp.where(i >= n_lat_tiles, 1, 0), 0, 0)),
            _resident((3, D_MODEL)),
            _resident((D_MODEL, 2 * D_FF)),
            _resident((D_FF, D_MODEL)),
        ],
        out_specs=pl.BlockSpec((None, ROW_TILE, D_MODEL), lambda b, i: (b, i, 0)),
        out_shape=jax.ShapeDtypeStruct((bsz, n_rows, D_MODEL), F32),
        compiler_params=_cparams(("parallel", "parallel")),
        name=f"ffn_half_{sub}",
    )(*x_args, modsel, nw, win, wout)


def _prep_kernel(x_ref, mod_ref, nw_ref, w_ref, rope_ref, seg32_ref, seg64_ref, pv_ref, wuq_ref, wukv_ref,
                 qa_ref, ka_ref, va_ref, qb_ref, kb_ref, vb_ref, qd_ref, kd_ref, vd_ref, zc_ref):
    x = x_ref[...]
    m = mod_ref[...]
    h = (_rms_rows(x) * nw_ref[1:2]) * (1.0 + m[4:5]) + m[3:4]
    hb = h.astype(BF16)
    seg32 = seg32_ref[...]
    seg64 = seg64_ref[...]
    cos32, sin32 = rope_ref[:, 0:128], rope_ref[:, 128:256]
    cos64, sin64 = rope_ref[:, 256:384], rope_ref[:, 384:512]

    def pv(lo, hi):
        return pv_ref[:, lo:hi]

    za = _dot(hb, w_ref[:, 0:A_COLS])
    qa = _rope(_seg_norm(za[:, 0:256], seg32, 32) * pv(P_GAQ, P_GAQ + 256), cos32, sin32, 16)
    qa = qa * (DA_QK_DIM ** -0.5 * LOG2E)
    ka = _rope(_seg_norm(za[:, 256:512], seg32, 32) * pv(P_GAK, P_GAK + 256), cos32, sin32, 16)
    qa_t = jnp.transpose(qa)
    for hh in range(2 * DA_HEADS):
        qa_ref[hh] = qa_t[32 * hh:32 * hh + 32, :].astype(BF16)
        ka_ref[hh] = ka[:, 32 * hh:32 * hh + 32].astype(BF16)

    def put_v(ref, hh, val_t):
        ref[hh, 0:HEAD_DIM, :] = val_t.astype(BF16)
        if V_ROWS > HEAD_DIM:
            ref[hh, HEAD_DIM:V_ROWS, :] = jnp.zeros((V_ROWS - HEAD_DIM, val_t.shape[1]), BF16)

    va_t = jnp.transpose(za[:, 512:768])
    for hh in range(DA_HEADS):
        put_v(va_ref, hh, va_t[64 * hh:64 * hh + 64, :])

    zb = _dot(hb, w_ref[:, A_COLS:A_COLS + B_PAD])
    cq = _rms_rows(zb[:, 0:256]) * pv(P_QN, P_QN + 256)
    q2 = _dot(cq.astype(BF16), wuq_ref[...])
    ckv = _rms_rows(zb[:, 256:384]) * pv(P_KVN, P_KVN + 128)
    kv2 = _dot(ckv.astype(BF16), wukv_ref[...])
    sb = (MLA_NOPE_DIM + MLA_ROPE_DIM) ** -0.5 * LOG2E
    qn = _seg_norm(q2[:, 0:256], seg64, 64) * pv(P_NOPE0, P_NOPE0 + 256) * sb
    kn = _seg_norm(kv2[:, 0:256], seg64, 64) * pv(P_NOPE1, P_NOPE1 + 256)
    seg32s = seg32[0:128, 0:128]
    qr = _rope(_seg_norm(q2[:, 256:384], seg32s, 32) * pv(P_ROPE0, P_ROPE0 + 128), cos32, sin32, 16) * sb
    kr = _rope(_seg_norm(zb[:, 384:512], seg32s, 32) * pv(P_ROPE1, P_ROPE1 + 128), cos32, sin32, 16)
    zpad = jnp.zeros((x.shape[0], 32), BF16)
    zpad_t = jnp.zeros((32, x.shape[0]), BF16)
    qn_t = jnp.transpose(qn)
    qr_t = jnp.transpose(qr)
    vb_t = jnp.transpose(kv2[:, 256:512])
    for hh in range(MLA_HEADS):
        qb_ref[hh, 0:64, :] = qn_t[64 * hh:64 * hh + 64, :].astype(BF16)
        qb_ref[hh, 64:96, :] = qr_t[32 * hh:32 * hh + 32, :].astype(BF16)
        qb_ref[hh, 96:128, :] = zpad_t
        kb_ref[hh, :, 0:64] = kn[:, 64 * hh:64 * hh + 64].astype(BF16)
        kb_ref[hh, :, 64:96] = kr[:, 0:32].astype(BF16)
        kb_ref[hh, :, 96:128] = zpad
        put_v(vb_ref, hh, vb_t[64 * hh:64 * hh + 64, :])

    zc_ref[...] = _dot(hb, w_ref[:, A_COLS + B_PAD:A_COLS + B_PAD + C_COLS])

    zd = _dot(hb, w_ref[:, A_COLS + B_PAD + C_COLS:W_COLS])
    qd = _rope(_seg_norm(zd[:, 0:256], seg64, 64) * pv(P_GDQ, P_GDQ + 256), cos64, sin64, 32)
    qd = qd * (HEAD_DIM ** -0.5 * LOG2E)
    kd = _rope(_seg_norm(zd[:, 256:384], seg64[0:128, 0:128], 64) * pv(P_GDK, P_GDK + 128), cos64, sin64, 32)
    qd_t = jnp.transpose(qd)
    vd_t = jnp.transpose(zd[:, 384:512])
    for hh in range(GQ_HEADS):
        qd_ref[hh] = qd_t[64 * hh:64 * hh + 64, :].astype(BF16)
    for hh in range(GQ_KV_HEADS):
        kd_ref[hh] = kd[:, 64 * hh:64 * hh + 64].astype(BF16)
        put_v(vd_ref, hh, vd_t[64 * hh:64 * hh + 64, :])


def _prep(xs, modsel, nw, w, rope, seg32, seg64, pvec, wuq, wukv, *, n_lat_tiles):
    bsz, t, _ = xs.shape
    n_tiles = pl.cdiv(t, ROW_TILE)

    def heads(h, d):
        return (pl.BlockSpec((None, h, ROW_TILE, d), lambda b, i: (b, 0, i, 0)),
                jax.ShapeDtypeStruct((bsz, h, t, d), BF16))

    def heads_t(h, d):
        return (pl.BlockSpec((None, h, d, ROW_TILE), lambda b, i: (b, 0, 0, i)),
                jax.ShapeDtypeStruct((bsz, h, d, t), BF16))

    outs = [heads_t(8, 32), heads(8, 32), heads_t(4, V_ROWS),
            heads_t(4, 128), heads(4, 128), heads_t(4, V_ROWS),
            heads_t(4, 64), heads(2, 64), heads_t(2, V_ROWS),
            (pl.BlockSpec((None, ROW_TILE, C_COLS), lambda b, i: (b, i, 0)),
             jax.ShapeDtypeStruct((bsz, t, C_COLS), F32))]
    return pl.pallas_call(
        _prep_kernel,
        grid=(bsz, n_tiles),
        in_specs=[
            pl.BlockSpec((None, ROW_TILE, D_MODEL), lambda b, i: (b, i, 0)),
            pl.BlockSpec((None, None, N_MOD, D_MODEL),
                         lambda b, i: (b, jnp.where(i >= n_lat_tiles, 1, 0), 0, 0)),
            _resident((3, D_MODEL)),
            _resident((D_MODEL, W_COLS)),
            pl.BlockSpec((ROW_TILE, 512), lambda b, i: (i, 0)),
            _resident((256, 256)),
            _resident((256, 256)),
            _resident((1, P_LEN)),
            _resident((MLA_Q_RANK, 384)),
            _resident((MLA_KV_RANK, 512)),
        ],
        out_specs=[o[0] for o in outs],
        out_shape=[o[1] for o in outs],
        compiler_params=_cparams(("parallel", "parallel")),
        name="mix_in_prep",
    )(xs, modsel, nw, w, rope, seg32, seg64, pvec, wuq, wukv)


def _attn_kernel(flag_ref, q_ref, k_ref, v_ref, kc_ref, vc_ref, o_ref, acc_scr, l_scr, m_scr, *, hq, hk, hv, n_kv):
    j = pl.program_id(3)
    fast = flag_ref[0] == 1

    @pl.when(j == 0)
    def _():
        acc_scr[...] = jnp.zeros(acc_scr.shape, F32)
        l_scr[...] = jnp.zeros(l_scr.shape, F32)
        m_scr[...] = jnp.full(m_scr.shape, -jnp.inf, F32)

    def scores(h, kr):
        return _dot(kr[h // (hq // hk)], q_ref[h])

    def step_fast(kr, vr):
        for h in range(hq):
            p = jnp.exp2(scores(h, kr))
            l_scr[h] += jnp.sum(p, axis=0, keepdims=True)
            acc_scr[h] += _dot(vr[h // (hq // hv)], p.astype(BF16))

    def step_safe(kr, vr):
        for h in range(hq):
            s = scores(h, kr)
            m_prev = m_scr[h]
            m_new = jnp.maximum(m_prev, jnp.max(s, axis=0, keepdims=True))
            alpha = jnp.exp2(m_prev - m_new)
            p = jnp.exp2(s - m_new[0:1, :])
            l_scr[h] = alpha * l_scr[h] + jnp.sum(p, axis=0, keepdims=True)
            acc_scr[h] = alpha[0:1, :] * acc_scr[h] + _dot(vr[h // (hq // hv)], p.astype(BF16))
            m_scr[h] = m_new

    def both(kr, vr):
        @pl.when(fast)
        def _():
            step_fast(kr, vr)

        @pl.when(jnp.logical_not(fast))
        def _():
            step_safe(kr, vr)

    @pl.when(j < n_kv)
    def _():
        both(k_ref, v_ref)

    @pl.when(j == n_kv)
    def _():
        both(kc_ref, vc_ref)
        outs = []
        for h in range(hq):
            outs.append(acc_scr[h][0:HEAD_DIM, :] / l_scr[h][0:1, :])
        o_ref[...] = jnp.transpose(jnp.concatenate(outs, axis=0))


def _att_key_tile(hq, n_lat):
    tk = ATT_TK_MAX
    while tk > LANES and (hq * ATT_TQ * tk * 6 > VMEM_LIMIT or n_lat % tk):
        tk //= 2
    return tk


def _attention(flag, q, k, v, *, n_lat, n_ctx, latent):
    bsz, hq_all, dq, t = q.shape
    n_grp = hq_all // ATT_HEADS
    hq = ATT_HEADS
    hk, hv = k.shape[1] // n_grp, v.shape[1] // n_grp
    width = hq * HEAD_DIM
    ctx_blk = n_lat // n_ctx
    tk = _att_key_tile(hq, n_lat)
    if latent:
        tq, nq, n_kv, q_off = ATT_TQ, n_lat // ATT_TQ, n_lat // tk, 0
    else:
        tq, nq, n_kv, q_off = n_ctx, 1, 0, ctx_blk

    def kv_blk(j):
        return jnp.minimum(j, max(n_kv - 1, 0))

    in_specs = [
        pl.BlockSpec((None, hq, dq, tq), lambda b, g, i, j, f: (b, g, 0, i + q_off)),
        pl.BlockSpec((None, hk, tk, dq), lambda b, g, i, j, f: (b, g, kv_blk(j), 0)),
        pl.BlockSpec((None, hv, V_ROWS, tk), lambda b, g, i, j, f: (b, g, 0, kv_blk(j))),
        pl.BlockSpec((None, hk, n_ctx, dq), lambda b, g, i, j, f: (b, g, ctx_blk, 0)),
        pl.BlockSpec((None, hv, V_ROWS, n_ctx), lambda b, g, i, j, f: (b, g, 0, ctx_blk)),
    ]
    grid_spec = pltpu.PrefetchScalarGridSpec(
        num_scalar_prefetch=1,
        grid=(bsz, n_grp, nq, n_kv + 1),
        in_specs=in_specs,
        out_specs=pl.BlockSpec((None, tq, width), lambda b, g, i, j, f: (b, i, g)),
        scratch_shapes=[pltpu.VMEM((hq, V_ROWS, tq), F32), pltpu.VMEM((hq, 8, tq), F32),
                        pltpu.VMEM((hq, 8, tq), F32)],
    )
    return pl.pallas_call(
        functools.partial(_attn_kernel, hq=hq, hk=hk, hv=hv, n_kv=n_kv),
        grid_spec=grid_spec,
        out_shape=jax.ShapeDtypeStruct((bsz, nq * tq, hq_all * HEAD_DIM), F32),
        compiler_params=_cparams(("parallel", "parallel", "parallel", "arbitrary")),
        name=f"flash_attn_h{hq_all}_d{dq}_{'lat' if latent else 'ctx'}",
    )(flag, q, k, v, k, v)


def _attend(flag, q, k, v, *, n_lat, n_ctx, need_ctx):
    o_lat = _attention(flag, q, k, v, n_lat=n_lat, n_ctx=n_ctx, latent=True)
    o_ctx = _attention(flag, q, k, v, n_lat=n_lat, n_ctx=n_ctx, latent=False) if need_ctx else None
    return o_lat, o_ctx


def _hgrn_block(zq_ref, zx_ref, zi_ref, lb_ref, tri_ref, negm_ref, ones_ref, segh_ref, segg_ref, o_ref, st_scr, oi_scr,
                *, rev):
    ng = HG_BLOCK // HG_SUB
    x2 = zx_ref[...] * LOG2E
    lb = lb_ref[...]
    log_lb, log_1m, one_m = lb[0:1], lb[1:2], lb[2:3]
    e = jnp.exp2(-jnp.abs(x2))
    rcp = 1.0 / (1.0 + e)
    log_sig = jnp.minimum(x2, 0.0) - jnp.log2(1.0 + e)
    b = log_1m + log_sig
    big = jnp.maximum(log_lb, b)
    logf = big + jnp.log2(1.0 + jnp.exp2(-jnp.abs(log_lb - b)))
    kk = one_m * (jnp.where(x2 >= 0.0, e, 1.0) * rcp)
    zq = zq_ref[...]
    q = zq / (1.0 + jnp.exp2(zq * -LOG2E))
    v = zi_ref[:, 0:GROUP_WIDTH]

    h1 = logf.astype(BF16)
    r1 = logf - h1.astype(F32)
    h2 = r1.astype(BF16)
    h3 = (r1 - h2.astype(F32)).astype(BF16)
    tri = tri_ref[...]
    ones = ones_ref[...]
    g = _dot(tri, h1) + _dot(tri, h2) + _dot(tri, h3)
    gtot = _dot(ones, h1) + _dot(ones, h2) + _dot(ones, h3)
    qdec = jnp.exp2(g)
    kdec = jnp.exp2(gtot - g)
    dec = qdec * kdec
    qtb = (q * qdec).astype(BF16)
    ktb = (kk * kdec).astype(BF16)
    vb = v.astype(BF16)

    heads = [(slice(HG_EXPAND * hh, HG_EXPAND * (hh + 1)), slice(HG_VDIM * hh, HG_VDIM * (hh + 1)))
             for hh in range(HG_HEADS)]
    upd = []
    for n in range(ng):
        r = slice(n * HG_SUB, (n + 1) * HG_SUB)
        upd.append(jnp.concatenate(
            [lax.dot_general(vb[r, vs], ktb[r, ks], (((0,), (0,)), ((), ())), preferred_element_type=F32)
             for ks, vs in heads], axis=1))
    st = st_scr[...]
    inter = [None] * ng
    for n in (range(ng - 1, -1, -1) if rev else range(ng)):
        r = slice(n * HG_SUB, (n + 1) * HG_SUB)
        stb = st.astype(BF16)
        inter[n] = jnp.concatenate(
            [lax.dot_general(qtb[r, ks], stb[:, ks], (((1,), (1,)), ((), ())), preferred_element_type=F32)
             for ks, _ in heads], axis=1)
        st = st * dec[n * HG_SUB:n * HG_SUB + 1, :] + upd[n]
    st_scr[...] = st

    g3 = g.reshape(ng, HG_SUB, HG_FDIM)
    k3 = kk.reshape(ng, HG_SUB, HG_FDIM)
    q3 = q.reshape(ng, HG_SUB, HG_FDIM)
    v3 = v.reshape(ng, HG_SUB, GROUP_WIDTH)
    segh = segh_ref[...]
    half = HG_SUB // 2

    def span(t):
        if rev:
            return (half, HG_SUB) if t >= half else (0, HG_SUB)
        return (0, half) if t < half else (0, HG_SUB)

    for t0 in range(0, HG_SUB, HG_TGROUP):
        ts = range(t0, t0 + HG_TGROUP)
        wbs = []
        for t in ts:
            s0, s1 = span(t)
            tt = (t // half) * half
            nm = jnp.concatenate([negm_ref[t, tt:tt + half, :]] * (HG_FDIM // LANES), axis=1)[None]
            gt = g3[:, t:t + 1, :]
            parts = [gt - g3[:, a:a + half, :] for a in range(s0, s1, half)]
            parts = [p + nm if a == tt else p for a, p in zip(range(s0, s1, half), parts)]
            diff = parts[0] if len(parts) == 1 else jnp.concatenate(parts, axis=1)
            w = jnp.exp2(diff) * k3[:, s0:s1, :] * q3[:, t:t + 1, :]
            wbs.append(w.reshape(ng * (s1 - s0), HG_FDIM).astype(BF16))
        wb = jnp.concatenate(wbs, axis=0)
        hr = wb.shape[0] // 2
        cexp = jnp.concatenate([_dot(wb[0:hr], segh), _dot(wb[hr:], segh)], axis=0)
        row = 0
        for t in ts:
            s0, s1 = span(t)
            nr = ng * (s1 - s0)
            prod = (cexp[row:row + nr] * v3[:, s0:s1, :].reshape(nr, GROUP_WIDTH)).astype(BF16)
            row += nr
            segg = segg_ref[0 if s1 - s0 == half else 1]
            res = _dot(segg[:, 0:nr], prod)
            for c in range(GROUP_WIDTH // LANES):
                oi_scr[c][pl.ds(t, ng, stride=HG_SUB), :] = res[:, c * LANES:(c + 1) * LANES]
    intra = jnp.concatenate([oi_scr[c][...] for c in range(GROUP_WIDTH // LANES)], axis=1)
    o_ref[...] = jnp.concatenate(inter, axis=0) + intra


def _hgrn_kernel(zqf_ref, zxf_ref, zif_ref, zqb_ref, zxb_ref, zib_ref, lb_ref, tri_ref, negm_ref, ones_ref, segh_ref,
                 segg_ref, of_ref, ob_ref, stf_scr, stb_scr, *oi_scr):
    j = pl.program_id(1)

    @pl.when(j == 0)
    def _():
        stf_scr[...] = jnp.zeros(stf_scr.shape, F32)
        stb_scr[...] = jnp.zeros(stb_scr.shape, F32)

    n_oi = GROUP_WIDTH // LANES
    _hgrn_block(zqf_ref, zxf_ref, zif_ref, lb_ref.at[0], tri_ref.at[0], negm_ref.at[0], ones_ref, segh_ref, segg_ref,
                of_ref, stf_scr, oi_scr[0:n_oi], rev=False)
    _hgrn_block(zqb_ref, zxb_ref, zib_ref, lb_ref.at[1], tri_ref.at[1], negm_ref.at[1], ones_ref, segh_ref, segg_ref,
                ob_ref, stb_scr, oi_scr[n_oi:2 * n_oi], rev=True)


def _hgrn(zc, lbp, tri, negm, ones, segh, segg, *, n_lat, n_ctx):
    bsz, t, _ = zc.shape
    nl = n_lat // HG_BLOCK
    nc = n_ctx // HG_BLOCK

    def blk_f(j):
        return jnp.where(j < nc, nl + j, j - nc)

    def blk_b(j):
        return jnp.where(j < nc, nl + nc - 1 - j, nl - 1 - (j - nc))

    def col(blk, c):
        return pl.BlockSpec((None, HG_BLOCK, HG_FDIM), lambda b, j: (b, blk(j), c))

    out = jax.ShapeDtypeStruct((bsz, t, GROUP_WIDTH), F32)
    return pl.pallas_call(
        _hgrn_kernel,
        grid=(bsz, nl + nc),
        in_specs=[
            col(blk_f, 0), col(blk_f, 1), col(blk_f, 3),
            col(blk_b, 0), col(blk_b, 2), col(blk_b, 3),
            _resident((2, 8, HG_FDIM)),
            _resident((2, HG_BLOCK, HG_BLOCK)),
            _resident((2, HG_SUB, HG_SUB, LANES)),
            _resident((HG_BLOCK, HG_BLOCK)),
            _resident((HG_FDIM, GROUP_WIDTH)),
            _resident((2, HG_BLOCK // HG_SUB, HG_BLOCK)),
        ],
        out_specs=[pl.BlockSpec((None, HG_BLOCK, GROUP_WIDTH), lambda b, j: (b, blk_f(j), 0)),
                   pl.BlockSpec((None, HG_BLOCK, GROUP_WIDTH), lambda b, j: (b, blk_b(j), 0))],
        out_shape=[out, out],
        scratch_shapes=[pltpu.VMEM((HG_VDIM, HG_FDIM), F32), pltpu.VMEM((HG_VDIM, HG_FDIM), F32)]
        + [pltpu.VMEM((HG_BLOCK, LANES), F32) for _ in range(2 * (GROUP_WIDTH // LANES))],
        compiler_params=_cparams(("parallel", "arbitrary")),
        name="hgrn2_scan",
    )(zc, zc, zc, zc, zc, zc, lbp, tri, negm, ones, segh, segg)


def _out_kernel(*refs, n_lat_tiles, with_ctx):
    if with_ctx:
        (x_ref, mod_ref, oa_ref, ob_ref, od_ref, oa_c_ref, ob_c_ref, od_c_ref,
         ocf_ref, ocb_ref, g_ref, w_ref, pv_ref, seg64_ref, o_ref) = refs
        is_ctx = pl.program_id(1) >= n_lat_tiles
        oa = jnp.where(is_ctx, oa_c_ref[...], oa_ref[...])
        ob = jnp.where(is_ctx, ob_c_ref[...], ob_ref[...])
        od = jnp.where(is_ctx, od_c_ref[...], od_ref[...])
    else:
        x_ref, mod_ref, oa_ref, ob_ref, od_ref, ocf_ref, ocb_ref, g_ref, w_ref, pv_ref, seg64_ref, o_ref = refs
        oa, ob, od = oa_ref[...], ob_ref[...], od_ref[...]
    x = x_ref[...]
    m = mod_ref[...]
    seg64 = seg64_ref[...]
    lam = pv_ref[:, O_LAM:O_LAM + LANES]

    lane = lax.broadcasted_iota(jnp.int32, (x.shape[0], LANES), 1)
    ya = []
    for c in range(DA_HEADS):
        pair = oa[:, c * LANES:(c + 1) * LANES]
        diff = pair - lam * pltpu.roll(pair, HEAD_DIM, 1)
        ya.append(jnp.where(lane < HEAD_DIM, diff, 0.0))
    ya = jnp.concatenate(ya, axis=1)
    ya = jnp.concatenate([_seg_norm(ya[:, 0:256], seg64, 64), _seg_norm(ya[:, 256:512], seg64, 64)], axis=1)
    ya = ya * pv_ref[:, O_SUBLN:O_SUBLN + 512]

    oc = ocf_ref[...] + ocb_ref[...]
    yc = _seg_norm(oc, seg64, 64) * pv_ref[:, O_HGN:O_HGN + 256] * _silu(g_ref[...])

    acc = _dot(ya.astype(BF16), w_ref[0:512, :])
    acc = acc + _dot(ob.astype(BF16), w_ref[512:768, :])
    acc = acc + _dot(yc.astype(BF16), w_ref[768:1024, :])
    acc = acc + _dot(od.astype(BF16), w_ref[1024:1280, :])
    o_ref[...] = x + m[5:6] * acc


def _out_proj(xs, modsel, oa, ob, od, oc, zc, w, pvec, seg64, *, n_lat_tiles, n_rows):
    bsz = xs.shape[0]
    n_tiles = pl.cdiv(n_rows, ROW_TILE)
    with_ctx = n_tiles > n_lat_tiles
    row = lambda width: pl.BlockSpec((None, ROW_TILE, width), lambda b, i: (b, i, 0))
    lat = lambda width: pl.BlockSpec((None, ROW_TILE, width), lambda b, i: (b, jnp.minimum(i, n_lat_tiles - 1), 0))
    ctx = lambda width: pl.BlockSpec((None, ROW_TILE, width), lambda b, i: (b, jnp.maximum(i - n_lat_tiles, 0), 0))
    att_specs = [lat(512), lat(256), lat(256)]
    att_args = [oa[0], ob[0], od[0]]
    if with_ctx:
        att_specs += [ctx(512), ctx(256), ctx(256)]
        att_args += [oa[1], ob[1], od[1]]
    return pl.pallas_call(
        functools.partial(_out_kernel, n_lat_tiles=n_lat_tiles, with_ctx=with_ctx),
        grid=(bsz, n_tiles),
        in_specs=[
            row(D_MODEL),
            pl.BlockSpec((None, None, N_MOD, D_MODEL),
                         lambda b, i: (b, jnp.where(i >= n_lat_tiles, 1, 0), 0, 0)),
            *att_specs,
            row(GROUP_WIDTH), row(GROUP_WIDTH),
            pl.BlockSpec((None, ROW_TILE, GROUP_WIDTH), lambda b, i: (b, i, C_COLS // GROUP_WIDTH - 1)),
            _resident((1280, D_MODEL)),
            _resident((1, O_LEN)),
            _resident((256, 256)),
        ],
        out_specs=row(D_MODEL),
        out_shape=jax.ShapeDtypeStruct((bsz, n_rows, D_MODEL), F32),
        compiler_params=_cparams(("parallel", "parallel")),
        name="mix_out_proj",
    )(xs, modsel, *att_args, oc[0], oc[1], zc, w, pvec, seg64)


def _block_diag_ones(n, size):
    idx = np.arange(n) // size
    return jnp.asarray((idx[:, None] == idx[None, :]).astype(np.float32), dtype=BF16)


def _rope_table(n_lat, n_ctx):
    rows = n_lat // GRID_W
    r = jnp.repeat(jnp.arange(rows), GRID_W).astype(F32)
    col = jnp.tile(jnp.arange(GRID_W), rows).astype(F32)

    def pattern(rot_dim):
        n_freq = rot_dim // 4
        inv = ROPE_THETA ** (-jnp.arange(n_freq, dtype=F32) / n_freq)
        ang = jnp.concatenate([r[:, None] * inv, col[:, None] * inv], axis=-1)
        cos, sin = jnp.cos(ang), jnp.sin(ang)
        cos_l = jnp.tile(jnp.concatenate([cos, cos], axis=-1), (1, LANES // rot_dim))
        sin_l = jnp.tile(jnp.concatenate([-sin, sin], axis=-1), (1, LANES // rot_dim))
        cos_l = jnp.concatenate([cos_l, jnp.ones((n_ctx, LANES), F32)], axis=0)
        sin_l = jnp.concatenate([sin_l, jnp.zeros((n_ctx, LANES), F32)], axis=0)
        return cos_l, sin_l

    c32, s32 = pattern(32)
    c64, s64 = pattern(64)
    return jnp.concatenate([c32, s32, c64, s64], axis=1)


def _tile_lanes(v, reps):
    return jnp.tile(v.astype(F32), reps)


def kernel(x, c, ctx, c_ctx, ada_w, ada_b, norm_w, ffn_w_in, ffn_w_out, mix_w_in, mix_w_out, da_qk_norm, da_lambda,
           da_subln, mla_q_norm, mla_kv_norm, mla_w_uq, mla_w_ukv, mla_nope_norm, mla_rope_norm, hg_lb_logits,
           hg_out_norm, gq_qk_norm):
    bsz, n_lat, _ = x.shape
    n_ctx = ctx.shape[1]
    depth = ada_w.shape[0]
    assert bsz + 1 <= 8 and n_lat % ATT_TQ == 0 and n_lat % n_ctx == 0
    assert n_lat % ROW_TILE == 0 and n_ctx % HG_BLOCK == 0
    t = n_lat + n_ctx
    n_lat_tiles = n_lat // ROW_TILE
    xs = x.astype(F32)

    cc =jnp.zeros((8, D_MODEL), F32).at[0:bsz].set(c).at[bsz].set(c_ctx)
    mods = _modulation(cc, ada_w, ada_b).reshape(depth, 8, N_MOD, D_MODEL)

    p = jax.nn.softmax(hg_lb_logits.astype(F32), axis=1)
    lbs = jnp.maximum(jnp.cumsum(p, axis=1) - p[:, :1], 0.0)

    rope = _rope_table(n_lat, n_ctx)
    seg32 = _block_diag_ones(256, 32)
    seg64 = _block_diag_ones(256, 64)
    ones_sub = _block_diag_ones(HG_BLOCK, HG_SUB)
    ii = np.arange(HG_BLOCK)
    same = (ii[:, None] // HG_SUB) == (ii[None, :] // HG_SUB)
    tri = jnp.asarray(np.stack([same & (ii[None, :] <= ii[:, None]), same & (ii[None, :] >= ii[:, None])])
                      .astype(np.float32), dtype=BF16)
    segh = jnp.asarray((np.arange(HG_FDIM)[:, None] // HG_EXPAND == np.arange(GROUP_WIDTH)[None, :] // HG_VDIM)
                       .astype(np.float32), dtype=BF16)
    n_steps = HG_BLOCK // HG_SUB
    segg_np = np.zeros((2, n_steps, HG_BLOCK), np.float32)
    segg_np[0, :, :HG_BLOCK // 2] = np.arange(n_steps)[:, None] == np.arange(HG_BLOCK // 2)[None, :] // (HG_SUB // 2)
    segg_np[1] = np.arange(n_steps)[:, None] == ii[None, :] // HG_SUB
    segg = jnp.asarray(segg_np, dtype=BF16)
    pp = np.arange(HG_SUB)
    negm_np = np.where(np.stack([pp[None, :] <= pp[:, None], pp[None, :] >= pp[:, None]]), 0.0, -np.inf)
    negm = jnp.asarray(np.broadcast_to(negm_np[..., None], (2, HG_SUB, HG_SUB, LANES)).astype(np.float32))

    def group_cols(w, first):
        r = w.shape[0]
        w3 = w.reshape(r, MLA_HEADS, -1)
        return jnp.concatenate([w3[:, :, :first].reshape(r, -1), w3[:, :, first:].reshape(r, -1)], axis=1)

    def fast_flag(qn2, kn2, scale):
        bound = jnp.sqrt(qn2 * kn2) * (scale * LOG2E * 1.02)
        return (bound <= FAST_SCORE_BOUND).astype(jnp.int32).reshape(1)

    def gmax2(g):
        return jnp.max(jnp.abs(g.astype(F32))) ** 2

    for l in range(depth):
        modsel = jnp.stack([mods[l, 0:bsz], jnp.broadcast_to(mods[l, bsz], (bsz, N_MOD, D_MODEL))], axis=1)
        nw = norm_w[l].astype(F32)
        last = l == depth - 1

        xs = _ffn(xs, ctx.astype(F32) if l == 0 else None, modsel, nw,
                  ffn_w_in[l, 0].astype(BF16), ffn_w_out[l, 0].astype(BF16),
                  sub=0, n_lat_tiles=n_lat_tiles, n_rows=t)

        wl = mix_w_in[l]
        w_in = jnp.concatenate(
            [wl[:, 0:A_COLS], wl[:, A_COLS:A_COLS + B_COLS], jnp.zeros((D_MODEL, B_PAD - B_COLS), F32),
             wl[:, A_COLS + B_COLS:]], axis=1).astype(BF16)
        pvec = jnp.concatenate([
            _tile_lanes(da_qk_norm[l, 0], 8), _tile_lanes(da_qk_norm[l, 1], 8),
            mla_q_norm[l].astype(F32), mla_kv_norm[l].astype(F32),
            _tile_lanes(mla_nope_norm[l, 0], 4), _tile_lanes(mla_nope_norm[l, 1], 4),
            _tile_lanes(mla_rope_norm[l, 0], 4), _tile_lanes(mla_rope_norm[l, 1], 4),
            _tile_lanes(gq_qk_norm[l, 0], 4), _tile_lanes(gq_qk_norm[l, 1], 2)]).reshape(1, P_LEN)
        qa, ka, va, qb, kb, vb, qd, kd, vd, zc = _prep(
            xs, modsel, nw, w_in, rope, seg32, seg64, pvec,
            group_cols(mla_w_uq[l], MLA_NOPE_DIM).astype(BF16), group_cols(mla_w_ukv[l], MLA_NOPE_DIM).astype(BF16),
            n_lat_tiles=n_lat_tiles)

        fa = fast_flag(DA_QK_DIM * gmax2(da_qk_norm[l, 0]), DA_QK_DIM * gmax2(da_qk_norm[l, 1]), DA_QK_DIM ** -0.5)
        fb = fast_flag(MLA_NOPE_DIM * gmax2(mla_nope_norm[l, 0]) + MLA_ROPE_DIM * gmax2(mla_rope_norm[l, 0]),
                       MLA_NOPE_DIM * gmax2(mla_nope_norm[l, 1]) + MLA_ROPE_DIM * gmax2(mla_rope_norm[l, 1]),
                       (MLA_NOPE_DIM + MLA_ROPE_DIM) ** -0.5)
        fd = fast_flag(HEAD_DIM * gmax2(gq_qk_norm[l, 0]), HEAD_DIM * gmax2(gq_qk_norm[l, 1]), HEAD_DIM ** -0.5)
        oa = _attend(fa, qa, ka, va, n_lat=n_lat, n_ctx=n_ctx, need_ctx=not last)
        ob = _attend(fb, qb, kb, vb, n_lat=n_lat, n_ctx=n_ctx, need_ctx=not last)
        od = _attend(fd, qd, kd, vd, n_lat=n_lat, n_ctx=n_ctx, need_ctx=not last)

        lb = lbs[:, l]
        lbp = jnp.zeros((2, 8, HG_FDIM), F32)
        lbp = lbp.at[:, 0].set(jnp.log2(lb)).at[:, 1].set(jnp.log1p(-lb) * LOG2E).at[:, 2].set(1.0 - lb)
        oc = _hgrn(zc, lbp, tri, negm, ones_sub, segh, segg, n_lat=n_lat, n_ctx=n_ctx)

        lam_init = 0.8 - 0.6 * math.exp(-0.3 * l)
        lp = da_lambda[l].astype(F32)
        lam = jnp.exp(jnp.sum(lp[0] * lp[1])) - jnp.exp(jnp.sum(lp[2] * lp[3])) + lam_init
        subln = jnp.concatenate([da_subln[l].astype(F32) * (1.0 - lam_init), jnp.zeros((HEAD_DIM,), F32)])
        opv = jnp.concatenate([jnp.tile(subln, DA_HEADS), _tile_lanes(hg_out_norm[l], HG_HEADS),
                               jnp.full((LANES,), lam, F32)]).reshape(1, O_LEN)
        wo = mix_w_out[l]
        wa = jnp.concatenate([wo[0:GROUP_WIDTH].reshape(DA_HEADS, HEAD_DIM, D_MODEL),
                              jnp.zeros((DA_HEADS, HEAD_DIM, D_MODEL), F32)], axis=1).reshape(2 * GROUP_WIDTH, D_MODEL)
        w_out = jnp.concatenate([wa, wo[GROUP_WIDTH:]], axis=0).astype(BF16)
        xs = _out_proj(xs, modsel, oa, ob, od, oc, zc, w_out, opv, seg64, n_lat_tiles=n_lat_tiles,
                       n_rows=n_lat if last else t)

        xs = _ffn(xs, None, modsel, nw, ffn_w_in[l, 1].astype(BF16), ffn_w_out[l, 1].astype(BF16),
                  sub=2, n_lat_tiles=n_lat_tiles, n_rows=n_lat if last else t)
    return xs
```
